```python
import math
import jax, jax.numpy as jnp
from jax import lax
import numpy as np

D_MODEL = 2048
BATCH = 4
SEQ = 2048
DEPTH = 2
DEC_BATCH = 128
DEC_SEQ = 4
PAST_LEN = 2048
PAGE_SIZE = 128

N_A = DEPTH // 2
N_B = DEPTH - N_A
CONV_W = 31
HEAD_DIM = 64
N_HEADS = D_MODEL // (2 * HEAD_DIM)
QK_DIM = 2 * HEAD_DIM
V_DIM = 2 * HEAD_DIM
D_FF = ((8 * D_MODEL + 3 * 256 - 1) // (3 * 256)) * 256
PLE_DIM = 256
ROPE_THETA = 10000.0
Q_BLOCK = 128
RMS_EPS = 1e-6
LN_EPS = 1e-5

kernel_name = 'yoco_conformer_diffattn_decode_step'


def _rmsnorm(x, g):
    xf = x.astype(jnp.float32)
    y = xf * lax.rsqrt(jnp.mean(xf * xf, axis=-1, keepdims=True) + RMS_EPS)
    return (y * g.astype(jnp.float32)).astype(x.dtype)


def _layernorm(x, g, b):
    xf = x.astype(jnp.float32)
    mu = jnp.mean(xf, axis=-1, keepdims=True)
    xc = xf - mu
    y = xc * lax.rsqrt(jnp.mean(xc * xc, axis=-1, keepdims=True) + LN_EPS)
    return (y * g.astype(jnp.float32) + b.astype(jnp.float32)).astype(x.dtype)


def _rope(x, pos):
    half = HEAD_DIM // 2
    inv = 1.0 / (ROPE_THETA ** (jnp.arange(half, dtype=jnp.float32) / half))
    ang = pos.astype(jnp.float32)[:, None] * inv[None, :]
    cos = jnp.cos(ang)[None, :, None, None, :]
    sin = jnp.sin(ang)[None, :, None, None, :]
    xf = x.astype(jnp.float32)
    x1, x2 = xf[..., :half], xf[..., half:]
    return jnp.concatenate([x1 * cos - x2 * sin, x2 * cos + x1 * sin], axis=-1).astype(x.dtype)


def _conformer_conv(u, buf, w_pw1, b_pw1, w_dw, b_dw, g_ln, b_ln, w_pw2, b_pw2):
    a, gt = jnp.split(u @ w_pw1 + b_pw1, 2, axis=-1)
    glu = a * jax.nn.sigmoid(gt)
    u_pad = jnp.concatenate([buf.astype(glu.dtype), glu], axis=1)
    c = lax.conv_general_dilated(
        u_pad, w_dw[:, None, :].astype(u_pad.dtype), window_strides=(1,), padding='VALID',
        dimension_numbers=('NWC', 'WIO', 'NWC'), feature_group_count=D_MODEL) + b_dw
    c = _layernorm(c, g_ln, b_ln)
    out = jax.nn.silu(c) @ w_pw2 + b_pw2
    return out, u_pad[:, -(CONV_W - 1):]


def _diff_attend(q, k, v, lam, mask):
    s = jnp.einsum('bqhcd,bkhcd->bhcqk', q, k).astype(jnp.float32) * (HEAD_DIM ** -0.5)
    s = jnp.where(mask, s, -jnp.inf)
    a = jax.nn.softmax(s, axis=-1)
    w = a[:, :, 0] - lam * a[:, :, 1]
    return jnp.einsum('bhqk,bkhe->bqhe', w.astype(v.dtype), v)


def _attend_prompt(q, k, v, lam):
    B, S = q.shape[0], q.shape[1]
    nq = S // Q_BLOCK
    qb = q.reshape(B, nq, Q_BLOCK, N_HEADS, 2, HEAD_DIM).transpose(1, 0, 2, 3, 4, 5)
    kpos = jnp.arange(S)

    def block(args):
        qi, bi = args
        qpos = bi * Q_BLOCK + jnp.arange(Q_BLOCK)
        return _diff_attend(qi, k, v, lam, kpos[None, :] <= qpos[:, None])

    out = lax.map(block, (qb, jnp.arange(nq)))
    return out.transpose(1, 0, 2, 3, 4).reshape(B, S, N_HEADS, V_DIM)


def _attend_sample(q, k, v, lam, past_len):
    T = q.shape[1]
    qpos = past_len + jnp.arange(T)
    kpos = jnp.arange(k.shape[1])
    return _diff_attend(q, k, v, lam, kpos[None, :] <= qpos[:, None])


def _trunk(x, p, pos, conv_state, past_k, past_v, W):
    B, T = x.shape[0], x.shape[1]
    h = x
    new_conv = []
    k_new = v_new = k_all = v_all = None
    for i in range(DEPTH):
        u = _rmsnorm(h, W['g_mix_pre'][i])
        if i < N_A:
            m, buf = _conformer_conv(u, conv_state[i], W['w_pw1'][i], W['b_pw1'][i], W['w_dw'][i],
                                     W['b_dw'][i], W['g_conv_ln'][i], W['b_conv_ln'][i],
                                     W['w_pw2'][i], W['b_pw2'][i])
            new_conv.append(buf)
        else:
            if k_new is None:
                s = _rmsnorm(h, W['g_kv_norm'])
                k_new = _rope((s @ W['w_k']).reshape(B, T, N_HEADS, 2, HEAD_DIM), pos)
                v_new = (s @ W['w_v']).reshape(B, T, N_HEADS, V_DIM)
                if past_k is None:
                    k_all, v_all = k_new, v_new
                else:
                    k_all = jnp.concatenate([past_k.astype(k_new.dtype), k_new], axis=1)
                    v_all = jnp.concatenate([past_v.astype(v_new.dtype), v_new], axis=1)
            b = i - N_A
            q = _rope((u @ W['w_q'][b]).reshape(B, T, N_HEADS, 2, HEAD_DIM), pos)
            lam_init = 0.8 - 0.6 * math.exp(-0.3 * i)
            f32 = jnp.float32
            lam = (jnp.exp(jnp.sum(W['lam_q1'][b].astype(f32) * W['lam_k1'][b].astype(f32)))
                   - jnp.exp(jnp.sum(W['lam_q2'][b].astype(f32) * W['lam_k2'][b].astype(f32)))
                   + lam_init)
            if past_k is None:
                o = _attend_prompt(q, k_all, v_all, lam)
            else:
                o = _attend_sample(q, k_all, v_all, lam, past_k.shape[1])
            o = _rmsnorm(o, W['g_subln'][b]) * (1.0 - lam_init)
            m = o.reshape(B, T, N_HEADS * V_DIM) @ W['w_o'][b]
        h = h + _rmsnorm(m, W['g_mix_post'][i])
        z = _rmsnorm(h, W['g_ffn_pre'][i])
        f = (jax.nn.silu(z @ W['w_ffn_gate'][i]) * (z @ W['w_ffn_up'][i])) @ W['w_ffn_down'][i]
        h = h + _rmsnorm(f, W['g_ffn_post'][i])
        e = p[i].astype(h.dtype) @ W['w_ple_proj'][i]
        h = h + _rmsnorm(jax.nn.sigmoid(h @ W['w_ple_gate'][i]) * e, W['g_ple_post'][i])
    return (h, jnp.stack(new_conv, axis=0),
            k_new.reshape(B, T, N_HEADS, QK_DIM), v_new)


def setup_inputs(seed: int = 0) -> dict:
    key = jax.random.key(seed)
    ks = jax.random.split(key, 40)
    f32 = jnp.float32

    def nrm(k, shape, scale=1.0):
        return jax.random.normal(k, shape, f32) * scale

    def gain(k, shape):
        return 1.0 + 0.01 * jax.random.normal(k, shape, f32)

    n_pages = PAST_LEN // PAGE_SIZE
    n_pool = (5 * DEC_BATCH * n_pages) // 4
    page_table = jax.random.permutation(ks[7], n_pool)[:DEC_BATCH * n_pages]
    page_table = page_table.reshape(DEC_BATCH, n_pages).astype(jnp.int32)
    D = D_MODEL
    return {
        'x_prompt': nrm(ks[0], (BATCH, SEQ, D)),
        'x_sample': nrm(ks[1], (DEC_BATCH, DEC_SEQ, D)),
        'p_prompt': nrm(ks[2], (DEPTH, BATCH, SEQ, PLE_DIM)),
        'p_sample': nrm(ks[3], (DEPTH, DEC_BATCH, DEC_SEQ, PLE_DIM)),
        'state_conv': nrm(ks[4], (N_A, DEC_BATCH, CONV_W - 1, D), 0.5),
        'cache_k': nrm(ks[5], (n_pool, PAGE_SIZE, N_HEADS, QK_DIM)),
        'cache_v': nrm(ks[6], (n_pool, PAGE_SIZE, N_HEADS, V_DIM)),
        'page_table': page_table,
        'g_mix_pre': gain(ks[8], (DEPTH, D)),
        'g_mix_post': gain(ks[9], (DEPTH, D)),
        'g_ffn_pre': gain(ks[10], (DEPTH, D)),
        'g_ffn_post': gain(ks[11], (DEPTH, D)),
        'g_ple_post': gain(ks[12], (DEPTH, D)),
        'w_pw1': nrm(ks[13], (N_A, D, 2 * D), D ** -0.5),
        'b_pw1': nrm(ks[14], (N_A, 2 * D), 0.01),
        'w_dw': nrm(ks[15], (N_A, CONV_W, D), CONV_W ** -0.5),
        'b_dw': nrm(ks[16], (N_A, D), 0.01),
        'g_conv_ln': gain(ks[17], (N_A, D)),
        'b_conv_ln': nrm(ks[18], (N_A, D), 0.01),
        'w_pw2': nrm(ks[19], (N_A, D, D), D ** -0.5),
        'b_pw2': nrm(ks[20], (N_A, D), 0.01),
        'g_kv_norm': gain(ks[21], (D,)),
        'w_k': nrm(ks[22], (D, N_HEADS * QK_DIM), D ** -0.5),
        'w_v': nrm(ks[23], (D, N_HEADS * V_DIM), D ** -0.5),
        'w_q': nrm(ks[24], (N_B, D, N_HEADS * QK_DIM), D ** -0.5),
        'lam_q1': nrm(ks[25], (N_B, HEAD_DIM), 0.1),
        'lam_k1': nrm(ks[26], (N_B, HEAD_DIM), 0.1),
        'lam_q2': nrm(ks[27], (N_B, HEAD_DIM), 0.1),
        'lam_k2': nrm(ks[28], (N_B, HEAD_DIM), 0.1),
        'g_subln': gain(ks[29], (N_B, V_DIM)),
        'w_o': nrm(ks[30], (N_B, N_HEADS * V_DIM, D), (N_HEADS * V_DIM) ** -0.5),
        'w_ffn_gate': nrm(ks[31], (DEPTH, D, D_FF), D ** -0.5),
        'w_ffn_up': nrm(ks[32], (DEPTH, D, D_FF), D ** -0.5),
        'w_ffn_down': nrm(ks[33], (DEPTH, D_FF, D), D_FF ** -0.5),
        'w_ple_proj': nrm(ks[34], (DEPTH, PLE_DIM, D), PLE_DIM ** -0.5),
        'w_ple_gate': nrm(ks[35], (DEPTH, D, D), D ** -0.5),
    }


def reference(x_prompt, x_sample, p_prompt, p_sample, state_conv, cache_k, cache_v, page_table,
              g_mix_pre, g_mix_post, g_ffn_pre, g_ffn_post, g_ple_post,
              w_pw1, b_pw1, w_dw, b_dw, g_conv_ln, b_conv_ln, w_pw2, b_pw2,
              g_kv_norm, w_k, w_v, w_q, lam_q1, lam_k1, lam_q2, lam_k2, g_subln, w_o,
              w_ffn_gate, w_ffn_up, w_ffn_down, w_ple_proj, w_ple_gate):
    W = {
        'g_mix_pre': g_mix_pre, 'g_mix_post': g_mix_post, 'g_ffn_pre': g_ffn_pre,
        'g_ffn_post': g_ffn_post, 'g_ple_post': g_ple_post,
        'w_pw1': w_pw1, 'b_pw1': b_pw1, 'w_dw': w_dw, 'b_dw': b_dw,
        'g_conv_ln': g_conv_ln, 'b_conv_ln': b_conv_ln, 'w_pw2': w_pw2, 'b_pw2': b_pw2,
        'g_kv_norm': g_kv_norm, 'w_k': w_k, 'w_v': w_v, 'w_q': w_q,
        'lam_q1': lam_q1, 'lam_k1': lam_k1, 'lam_q2': lam_q2, 'lam_k2': lam_k2,
        'g_subln': g_subln, 'w_o': w_o,
        'w_ffn_gate': w_ffn_gate, 'w_ffn_up': w_ffn_up, 'w_ffn_down': w_ffn_down,
        'w_ple_proj': w_ple_proj, 'w_ple_gate': w_ple_gate,
    }
    B, S = x_prompt.shape[0], x_prompt.shape[1]
    conv0 = jnp.zeros((N_A, B, CONV_W - 1, D_MODEL), x_prompt.dtype)
    y_p, conv_p, k_p, v_p = _trunk(x_prompt, p_prompt, jnp.arange(S), conv0, None, None, W)

    DB, T = x_sample.shape[0], x_sample.shape[1]
    past_len = page_table.shape[1] * cache_k.shape[1]
    past_k = cache_k[page_table].reshape(DB, past_len, N_HEADS, 2, HEAD_DIM)
    past_v = cache_v[page_table].reshape(DB, past_len, N_HEADS, V_DIM)
    pos_s = past_len + jnp.arange(T)
    y_s, conv_s, k_s, v_s = _trunk(x_sample, p_sample, pos_s, state_conv, past_k, past_v, W)
    return (y_p, y_s, conv_p, k_p, v_p, conv_s, k_s, v_s)
```

```python
import functools
import math

import jax
import jax.numpy as jnp
from jax import lax
from jax.experimental import pallas as pl
from jax.experimental.pallas import tpu as pltpu

F32 = jnp.float32
BF16 = jnp.bfloat16

RMS_EPS = 1e-6
LN_EPS = 1e-5
ROPE_THETA = 10000.0
HEAD_DIM = 64
HEAD_W = 2 * HEAD_DIM
ROPE_HALF = HEAD_DIM // 2
CONV_W = 31
CONV_HIST = CONV_W - 1
CONV_HALO = 32
SUBLANES = 8
LANES = 128
V7X_VMEM_LIMIT = 56 * 1024 * 1024


def _params(*sem):
    return pltpu.CompilerParams(dimension_semantics=sem, vmem_limit_bytes=V7X_VMEM_LIMIT)


def _sigmoid(x):
    return 1.0 / (1.0 + jnp.exp(-x))


def _rms_scale(x):
    return x * lax.rsqrt(jnp.mean(x * x, axis=-1, keepdims=True) + RMS_EPS)


def _tile(m, want):
    t = min(m, want)
    assert m % t == 0, (m, t)
    return t


def _pw1_glu_kernel(x_ref, g_ref, wa_ref, wg_ref, ba_ref, bg_ref, o_ref, u_ref):
    @pl.when(pl.program_id(1) == 0)
    def _():
        u_ref[...] = (_rms_scale(x_ref[...]) * g_ref[...]).astype(BF16)

    u = u_ref[...]
    a = jnp.dot(u, wa_ref[...], preferred_element_type=F32) + ba_ref[...]
    gt = jnp.dot(u, wg_ref[...], preferred_element_type=F32) + bg_ref[...]
    o_ref[...] = a * _sigmoid(gt)


def _pw1_glu(x, g, w, b, *, tm, tn):
    m, d = x.shape
    tm, tn = _tile(m, tm), _tile(d, tn)
    nj = d // tn
    return pl.pallas_call(
        _pw1_glu_kernel,
        out_shape=jax.ShapeDtypeStruct((m, d), F32),
        grid=(m // tm, nj),
        in_specs=[
            pl.BlockSpec((tm, d), lambda i, j: (i, 0)),
            pl.BlockSpec((1, d), lambda i, j: (0, 0)),
            pl.BlockSpec((d, tn), lambda i, j: (0, j)),
            pl.BlockSpec((d, tn), lambda i, j: (0, j + nj)),
            pl.BlockSpec((1, tn), lambda i, j: (0, j)),
            pl.BlockSpec((1, tn), lambda i, j: (0, j + nj)),
        ],
        out_specs=pl.BlockSpec((tm, tn), lambda i, j: (i, j)),
        scratch_shapes=[pltpu.VMEM((tm, d), BF16)],
        compiler_params=_params("arbitrary", "arbitrary"),
        name="pw1_glu",
    )(x, g, w, w, b, b)


def _ln_silu(c, g, b):
    mu = jnp.mean(c, axis=-1, keepdims=True)
    xc = c - mu
    y = xc * lax.rsqrt(jnp.mean(xc * xc, axis=-1, keepdims=True) + LN_EPS) * g + b
    return y * _sigmoid(y)


CONV_ROWS_PER_ITER = 4 * SUBLANES


def _conv_prompt_kernel(cur_ref, prev_ref, w_ref, bdw_ref, gln_ref, bln_ref, o_ref, buf_ref, c_ref,
                        *, tiles_per_seq):
    tm, d = cur_ref.shape
    first = (pl.program_id(0) % tiles_per_seq) == 0
    lead = CONV_HALO - CONV_HIST
    groups = CONV_ROWS_PER_ITER // SUBLANES

    for lb in range(d // LANES):
        lanes = slice(lb * LANES, (lb + 1) * LANES)
        buf_ref[lb, 0:CONV_HALO, :] = jnp.where(first, 0.0, prev_ref[:, lanes])
        buf_ref[lb, CONV_HALO:, :] = cur_ref[:, lanes]
        taps = [jnp.broadcast_to(w_ref[j:j + 1, lanes], (SUBLANES, LANES)) for j in range(CONV_W)]
        bias = jnp.broadcast_to(bdw_ref[:, lanes], (SUBLANES, LANES))

        def body(r, carry, lb=lb, taps=taps, bias=bias):
            r0 = pl.multiple_of(r * CONV_ROWS_PER_ITER, CONV_ROWS_PER_ITER)
            for q in range(groups):
                base = r0 + (q // 2) * 2 * SUBLANES + (q % 2)
                acc = bias
                for j in range(CONV_W):
                    acc = acc + taps[j] * buf_ref[lb, pl.ds(base + lead + j, SUBLANES, stride=2), :]
                c_ref[lb, pl.ds(base, SUBLANES, stride=2), :] = acc
            return carry

        lax.fori_loop(0, tm // CONV_ROWS_PER_ITER, body, 0)

    c = c_ref[...]
    mu = jnp.sum(jnp.sum(c, axis=0), axis=-1, keepdims=True) / d
    xc = c - mu[None]
    var = jnp.sum(jnp.sum(xc * xc, axis=0), axis=-1, keepdims=True) / d
    y = xc * lax.rsqrt(var + LN_EPS)[None] * gln_ref[...] + bln_ref[...]
    act = y * _sigmoid(y)
    for lb in range(d // LANES):
        o_ref[:, lb * LANES:(lb + 1) * LANES] = act[lb].astype(BF16)


def _conv_prompt(glu, w_dw, b_dw, g_ln, b_ln, *, seq, tm):
    m, d = glu.shape
    tm = _tile(seq, tm)
    halo_blocks = tm // CONV_HALO
    nlb = d // LANES
    chan = pl.BlockSpec((nlb, 1, LANES), lambda i: (0, 0, 0))
    return pl.pallas_call(
        functools.partial(_conv_prompt_kernel, tiles_per_seq=seq // tm),
        out_shape=jax.ShapeDtypeStruct((m, d), BF16),
        grid=(m // tm,),
        in_specs=[
            pl.BlockSpec((tm, d), lambda i: (i, 0)),
            pl.BlockSpec((CONV_HALO, d), lambda i: (jnp.maximum(i * halo_blocks - 1, 0), 0)),
            pl.BlockSpec((CONV_W, d), lambda i: (0, 0)),
            pl.BlockSpec((1, d), lambda i: (0, 0)),
            chan,
            chan,
        ],
        out_specs=pl.BlockSpec((tm, d), lambda i: (i, 0)),
        scratch_shapes=[pltpu.VMEM((nlb, CONV_HALO + tm, LANES), F32), pltpu.VMEM((nlb, tm, LANES), F32)],
        compiler_params=_params("arbitrary"),
        name="conv_prompt",
    )(glu, glu, w_dw, b_dw, g_ln.reshape(nlb, 1, LANES), b_ln.reshape(nlb, 1, LANES))


def _conv_sample_kernel(st_ref, glu_ref, w_ref, bdw_ref, gln_ref, bln_ref, o_ref, ns_ref, pad_ref, c_ref,
                        *, t_new):
    nb = st_ref.shape[0]
    w = w_ref[...]
    for bi in range(nb):
        new = glu_ref[bi * t_new:(bi + 1) * t_new, :]
        pad_ref[0:CONV_HIST, :] = st_ref[bi]
        pad_ref[CONV_HIST:CONV_HIST + t_new, :] = new
        for t in range(t_new):
            c_ref[bi * t_new + t:bi * t_new + t + 1, :] = (
                jnp.sum(pad_ref[t:t + CONV_W, :] * w, axis=0, keepdims=True) + bdw_ref[...])
        ns_ref[bi] = pad_ref[t_new:t_new + CONV_HIST, :]
    o_ref[...] = _ln_silu(c_ref[...], gln_ref[...], bln_ref[...]).astype(BF16)


def _conv_sample(state, glu, w_dw, b_dw, g_ln, b_ln, *, nb):
    b, hist, d = state.shape
    t_new = glu.shape[0] // b
    nb = _tile(b, nb)
    rows = nb * t_new
    return pl.pallas_call(
        functools.partial(_conv_sample_kernel, t_new=t_new),
        out_shape=(jax.ShapeDtypeStruct((b * t_new, d), BF16), jax.ShapeDtypeStruct((b, hist, d), F32)),
        grid=(b // nb,),
        in_specs=[
            pl.BlockSpec((nb, hist, d), lambda i: (i, 0, 0)),
            pl.BlockSpec((rows, d), lambda i: (i, 0)),
            pl.BlockSpec((CONV_W, d), lambda i: (0, 0)),
            pl.BlockSpec((1, d), lambda i: (0, 0)),
            pl.BlockSpec((1, d), lambda i: (0, 0)),
            pl.BlockSpec((1, d), lambda i: (0, 0)),
        ],
        out_specs=(pl.BlockSpec((rows, d), lambda i: (i, 0)),
                   pl.BlockSpec((nb, hist, d), lambda i: (i, 0, 0))),
        scratch_shapes=[pltpu.VMEM((CONV_HIST + t_new + SUBLANES, d), F32), pltpu.VMEM((rows, d), F32)],
        compiler_params=_params("arbitrary"),
        name="conv_sample",
    )(state, glu, w_dw, b_dw, g_ln, b_ln)


def _mm_norm_res_kernel(*refs, has_bias):
    if has_bias:
        a_ref, w_ref, b_ref, g_ref, r_ref, o_ref = refs
    else:
        a_ref, w_ref, g_ref, r_ref, o_ref = refs
    m = jnp.dot(a_ref[...].astype(BF16), w_ref[...], preferred_element_type=F32)
    if has_bias:
        m = m + b_ref[...]
    o_ref[...] = r_ref[...] + _rms_scale(m) * g_ref[...]


def _mm_norm_res(a, w, bias, g, res, *, tm):
    m, k = a.shape
    d = w.shape[1]
    tm = _tile(m, tm)
    row = pl.BlockSpec((1, d), lambda i: (0, 0))
    has_bias = bias is not None
    in_specs = [pl.BlockSpec((tm, k), lambda i: (i, 0)), pl.BlockSpec((k, d), lambda i: (0, 0))]
    args = [a, w]
    if has_bias:
        in_specs.append(row)
        args.append(bias)
    in_specs += [row, pl.BlockSpec((tm, d), lambda i: (i, 0))]
    args += [g, res]
    return pl.pallas_call(
        functools.partial(_mm_norm_res_kernel, has_bias=has_bias),
        out_shape=jax.ShapeDtypeStruct((m, d), F32),
        grid=(m // tm,),
        in_specs=in_specs,
        out_specs=pl.BlockSpec((tm, d), lambda i: (i, 0)),
        compiler_params=_params("arbitrary"),
        name="mm_norm_res",
    )(*args)


def _ffn_kernel(h_ref, gpre_ref, wg_ref, wu_ref, wd_ref, gpost_ref, o_ref, z_ref, acc_ref):
    c = pl.program_id(1)

    @pl.when(c == 0)
    def _():
        z_ref[...] = (_rms_scale(h_ref[...]) * gpre_ref[...]).astype(BF16)

    z = z_ref[...]
    gate = jnp.dot(z, wg_ref[...], preferred_element_type=F32)
    up = jnp.dot(z, wu_ref[...], preferred_element_type=F32)
    act = (gate * _sigmoid(gate) * up).astype(BF16)
    part = jnp.dot(act, wd_ref[...], preferred_element_type=F32)

    @pl.when(c == 0)
    def _():
        acc_ref[...] = part

    @pl.when(c > 0)
    def _():
        acc_ref[...] += part

    @pl.when(c == pl.num_programs(1) - 1)
    def _():
        o_ref[...] = h_ref[...] + _rms_scale(acc_ref[...]) * gpost_ref[...]


def _ffn(h, g_pre, wg, wu, wd, g_post, *, tm, tf):
    m, d = h.shape
    f = wg.shape[1]
    tm, tf = _tile(m, tm), _tile(f, tf)
    row = pl.BlockSpec((1, d), lambda i, c: (0, 0))
    return pl.pallas_call(
        _ffn_kernel,
        out_shape=jax.ShapeDtypeStruct((m, d), F32),
        grid=(m // tm, f // tf),
        in_specs=[
            pl.BlockSpec((tm, d), lambda i, c: (i, 0)),
            row,
            pl.BlockSpec((d, tf), lambda i, c: (0, c)),
            pl.BlockSpec((d, tf), lambda i, c: (0, c)),
            pl.BlockSpec((tf, d), lambda i, c: (c, 0)),
            row,
        ],
        out_specs=pl.BlockSpec((tm, d), lambda i, c: (i, 0)),
        scratch_shapes=[pltpu.VMEM((tm, d), BF16), pltpu.VMEM((tm, d), F32)],
        compiler_params=_params("arbitrary", "arbitrary"),
        name="ffn",
    )(h, g_pre, wg, wu, wd, g_post)


def _ple_kernel(h_ref, p_ref, wgate_ref, wproj_ref, g_ref, o_ref):
    h = h_ref[...]
    gate = _sigmoid(jnp.dot(h.astype(BF16), wgate_ref[...], preferred_element_type=F32))
    e = jnp.dot(p_ref[...].astype(BF16), wproj_ref[...], preferred_element_type=F32)
    o_ref[...] = h + _rms_scale(gate * e) * g_ref[...]


def _ple(h, p, w_gate, w_proj, g, *, tm):
    m, d = h.shape
    pd = p.shape[1]
    tm = _tile(m, tm)
    return pl.pallas_call(
        _ple_kernel,
        out_shape=jax.ShapeDtypeStruct((m, d), F32),
        grid=(m // tm,),
        in_specs=[
            pl.BlockSpec((tm, d), lambda i: (i, 0)),
            pl.BlockSpec((tm, pd), lambda i: (i, 0)),
            pl.BlockSpec((d, d), lambda i: (0, 0)),
            pl.BlockSpec((pd, d), lambda i: (0, 0)),
            pl.BlockSpec((1, d), lambda i: (0, 0)),
        ],
        out_specs=pl.BlockSpec((tm, d), lambda i: (i, 0)),
        compiler_params=_params("arbitrary"),
        name="ple",
    )(h, p, w_gate, w_proj, g)


def _rope(y, cos, sin_signed):
    lane = lax.broadcasted_iota(jnp.int32, (y.shape[0], LANES), 1)
    low_half = (lane % HEAD_DIM) < ROPE_HALF
    out = []
    for lb in range(y.shape[1] // LANES):
        yb = y[:, lb * LANES:(lb + 1) * LANES]
        partner = jnp.where(low_half, pltpu.roll(yb, LANES - ROPE_HALF, 1), pltpu.roll(yb, ROPE_HALF, 1))
        out.append(yb * cos + partner * sin_signed)
    return out


def _qkv_kernel(h_ref, gu_ref, gs_ref, wq_ref, wk_ref, wv_ref, cos_ref, sin_ref,
                q_ref, k_ref, v_ref, u_ref, s_ref):
    @pl.when(pl.program_id(1) == 0)
    def _():
        r = _rms_scale(h_ref[...])
        u_ref[...] = (r * gu_ref[...]).astype(BF16)
        s_ref[...] = (r * gs_ref[...]).astype(BF16)

    cos, sin_signed = cos_ref[...], sin_ref[...]
    s = s_ref[...]
    q = _rope(jnp.dot(u_ref[...], wq_ref[...], preferred_element_type=F32), cos, sin_signed)
    k = _rope(jnp.dot(s, wk_ref[...], preferred_element_type=F32), cos, sin_signed)
    for lb in range(len(q)):
        lanes = slice(lb * LANES, (lb + 1) * LANES)
        q_ref[:, lanes] = (q[lb] * (HEAD_DIM ** -0.5)).astype(q_ref.dtype)
        k_ref[:, lanes] = k[lb]
    v_ref[...] = jnp.dot(s, wv_ref[...], preferred_element_type=F32)


def _qkv(h, g_u, g_s, wq, wk, wv, cos, sin_signed, *, tm, tn, q_dtype):
    m, d = h.shape
    tm, tn = _tile(m, tm), _tile(d, tn)
    row = pl.BlockSpec((1, d), lambda i, j: (0, 0))
    wcol = pl.BlockSpec((d, tn), lambda i, j: (0, j))
    tab = pl.BlockSpec((tm, LANES), lambda i, j: (i, 0))
    out = pl.BlockSpec((tm, tn), lambda i, j: (i, j))
    return pl.pallas_call(
        _qkv_kernel,
        out_shape=(jax.ShapeDtypeStruct((m, d), q_dtype), jax.ShapeDtypeStruct((m, d), F32),
                   jax.ShapeDtypeStruct((m, d), F32)),
        grid=(m // tm, d // tn),
        in_specs=[pl.BlockSpec((tm, d), lambda i, j: (i, 0)), row, row, wcol, wcol, wcol, tab, tab],
        out_specs=(out, out, out),
        scratch_shapes=[pltpu.VMEM((tm, d), BF16), pltpu.VMEM((tm, d), BF16)],
        compiler_params=_params("arbitrary", "arbitrary"),
        name="qkv_rope",
    )(h, g_u, g_s, wq, wk, wv, cos, sin_signed)


def _rope_tables(pos):
    inv = 1.0 / (ROPE_THETA ** (jnp.arange(ROPE_HALF, dtype=F32) / ROPE_HALF))
    ang = pos.astype(F32)[:, None] * inv[None, :]
    cos, sin = jnp.cos(ang), jnp.sin(ang)
    reps = LANES // HEAD_DIM
    cos_l = jnp.tile(jnp.concatenate([cos, cos], axis=-1), (1, reps))
    sin_l = jnp.tile(jnp.concatenate([-sin, sin], axis=-1), (1, reps))
    return cos_l, sin_l


def _lambda(lam_ref, lam_init):
    lam = lam_ref[...]
    d1 = jnp.sum(lam[0:1] * lam[1:2], axis=-1, keepdims=True)
    d2 = jnp.sum(lam[2:3] * lam[3:4], axis=-1, keepdims=True)
    return jnp.exp(d1) - jnp.exp(d2) + lam_init


def _attn_prompt_kernel(q_ref, k_ref, v_ref, lam_ref, gsub_ref, o_ref,
                        m_ref, l_ref, acc_ref, *, tk, lam_init):
    tq = q_ref.shape[0]
    i = pl.program_id(2)
    q = q_ref[...]
    lane = lax.broadcasted_iota(jnp.int32, q.shape, 1)
    zero = jnp.zeros_like(q)
    qc = (jnp.where(lane < HEAD_DIM, q, zero), jnp.where(lane >= HEAD_DIM, q, zero))
    m_ref[...] = jnp.full(m_ref.shape, -jnp.inf, F32)
    l_ref[...] = jnp.zeros(l_ref.shape, F32)
    acc_ref[...] = jnp.zeros(acc_ref.shape, F32)

    def step(j, masked):
        k0 = pl.multiple_of(j * tk, tk)
        kj = k_ref[pl.ds(k0, tk), :].astype(BF16)
        vj = v_ref[pl.ds(k0, tk), :].astype(BF16)
        for c in range(2):
            s = lax.dot_general(qc[c], kj, (((1,), (1,)), ((), ())), preferred_element_type=F32)
            if masked:
                qpos = i * tq + lax.broadcasted_iota(jnp.int32, s.shape, 0)
                kpos = k0 + lax.broadcasted_iota(jnp.int32, s.shape, 1)
                s = jnp.where(kpos <= qpos, s, -jnp.inf)
            m_prev = m_ref[c]
            m_new = jnp.maximum(m_prev, jnp.max(s, axis=-1, keepdims=True))
            alpha = jnp.exp(m_prev - m_new)
            p = jnp.exp(s - m_new)
            l_ref[c] = alpha * l_ref[c] + jnp.sum(p, axis=-1, keepdims=True)
            acc_ref[c] = alpha * acc_ref[c] + jnp.dot(p.astype(BF16), vj, preferred_element_type=F32)
            m_ref[c] = m_new

    full_tiles = (i * tq) // tk

    def body(j, carry):
        step(j, False)
        return carry

    lax.fori_loop(0, full_tiles, body, 0)
    for d in range(pl.cdiv(tq, tk)):
        step(full_tiles + d, True)

    lam = _lambda(lam_ref, lam_init)
    o = acc_ref[0] / l_ref[0] - lam * (acc_ref[1] / l_ref[1])
    o_ref[...] = (_rms_scale(o) * gsub_ref[...] * (1.0 - lam_init)).astype(o_ref.dtype)


def _attn_prompt(q, k, v, lam_vecs, g_sub, *, batch, seq, tq, tk, lam_init):
    m, d = q.shape
    heads = d // HEAD_W
    tq, tk = _tile(seq, tq), _tile(seq, tk)
    assert tq % tk == 0 or tk % tq == 0
    nq = seq // tq
    return pl.pallas_call(
        functools.partial(_attn_prompt_kernel, tk=tk, lam_init=lam_init),
        out_shape=jax.ShapeDtypeStruct((m, d), BF16),
        grid=(batch, heads, nq),
        in_specs=[
            pl.BlockSpec((tq, HEAD_W), lambda b, h, i: (b * nq + i, h)),
            pl.BlockSpec((seq, HEAD_W), lambda b, h, i: (b, h)),
            pl.BlockSpec((seq, HEAD_W), lambda b, h, i: (b, h)),
            pl.BlockSpec(lam_vecs.shape, lambda b, h, i: (0, 0)),
            pl.BlockSpec((1, HEAD_W), lambda b, h, i: (0, 0)),
        ],
        out_specs=pl.BlockSpec((tq, HEAD_W), lambda b, h, i: (b * nq + i, h)),
        scratch_shapes=[pltpu.VMEM((2, tq, 1), F32), pltpu.VMEM((2, tq, 1), F32),
                        pltpu.VMEM((2, tq, HEAD_W), F32)],
        compiler_params=_params("arbitrary", "arbitrary", "arbitrary"),
        name="attn_prompt",
    )(q, k, v, lam_vecs, g_sub)


def _attn_sample_kernel(pt_ref, q_ref, kn_ref, vn_ref, kc_ref, vc_ref, lam_ref, gsub_ref, o_ref,
                        qbd_ref, knew_ref, vnew_ref, m_ref, l_ref, acc_ref, *, lam_init):
    del pt_ref
    t_new, d = q_ref.shape[1], q_ref.shape[2]
    heads = d // HEAD_W
    rows = 2 * t_new * heads
    p_idx = pl.program_id(1)

    @pl.when(p_idx == 0)
    def _():
        q = q_ref[0]
        head = lax.broadcasted_iota(jnp.int32, (heads, d), 0)
        lane = lax.broadcasted_iota(jnp.int32, (heads, d), 1)
        for c in range(2):
            sel = (lane // HEAD_W == head) & ((lane // HEAD_DIM) % 2 == c)
            for t in range(t_new):
                r0 = (c * t_new + t) * heads
                qbd_ref[r0:r0 + heads, :] = jnp.where(
                    sel, jnp.broadcast_to(q[t:t + 1, :], (heads, d)), 0.0).astype(BF16)
        knew_ref[...] = jnp.zeros(knew_ref.shape, F32)
        vnew_ref[...] = jnp.zeros(vnew_ref.shape, F32)
        knew_ref[0:t_new, :] = kn_ref[0]
        vnew_ref[0:t_new, :] = vn_ref[0]
        m_ref[...] = jnp.full(m_ref.shape, -jnp.inf, F32)
        l_ref[...] = jnp.zeros(l_ref.shape, F32)
        acc_ref[...] = jnp.zeros(acc_ref.shape, F32)

    def update(kp, vp, new_tokens):
        s = lax.dot_general(qbd_ref[...], kp.astype(BF16), (((1,), (1,)), ((), ())),
                            preferred_element_type=F32)
        if new_tokens:
            t_row = (lax.broadcasted_iota(jnp.int32, s.shape, 0) // heads) % t_new
            key = lax.broadcasted_iota(jnp.int32, s.shape, 1)
            s = jnp.where(key <= t_row, s, -jnp.inf)
        m_prev = m_ref[...]
        m_new = jnp.maximum(m_prev, jnp.max(s, axis=-1, keepdims=True))
        alpha = jnp.exp(m_prev - m_new)
        p = jnp.exp(s - m_new)
        l_ref[...] = alpha * l_ref[...] + jnp.sum(p, axis=-1, keepdims=True)
        acc_ref[...] = alpha * acc_ref[...] + jnp.dot(p.astype(BF16), vp.astype(BF16),
                                                     preferred_element_type=F32)
        m_ref[...] = m_new

    update(kc_ref[0], vc_ref[0], False)

    @pl.when(p_idx == pl.num_programs(1) - 1)
    def _():
        update(knew_ref[...], vnew_ref[...], True)
        half = rows // 2
        lam = _lambda(lam_ref, lam_init)
        o = acc_ref[...] / l_ref[...]
        w = o[0:half] - lam * o[half:rows]
        head = lax.broadcasted_iota(jnp.int32, w.shape, 0) % heads
        lane = lax.broadcasted_iota(jnp.int32, w.shape, 1)
        w = jnp.where(lane // HEAD_W == head, w, 0.0)
        ms = jnp.sum(w * w, axis=-1, keepdims=True) * (1.0 / HEAD_W)
        w = w * lax.rsqrt(ms + RMS_EPS)
        for t in range(t_new):
            o_t = jnp.sum(w[t * heads:(t + 1) * heads], axis=0, keepdims=True)
            o_ref[0, t:t + 1, :] = o_t * gsub_ref[...] * (1.0 - lam_init)


def _attn_sample(q, k_new, v_new, cache_k, cache_v, page_table, lam_vecs, g_sub_tiled, *, lam_init):
    b, t_new, d = q.shape
    n_pages = page_table.shape[1]
    page = cache_k.shape[1]
    heads = d // HEAD_W
    rows = 2 * t_new * heads
    tok = pl.BlockSpec((1, t_new, d), lambda bi, p, pt: (bi, 0, 0))
    pg = pl.BlockSpec((1, page, d), lambda bi, p, pt: (pt[bi * n_pages + p], 0, 0))
    grid_spec = pltpu.PrefetchScalarGridSpec(
        num_scalar_prefetch=1,
        grid=(b, n_pages),
        in_specs=[tok, tok, tok, pg, pg,
                  pl.BlockSpec(lam_vecs.shape, lambda bi, p, pt: (0, 0)),
                  pl.BlockSpec((1, d), lambda bi, p, pt: (0, 0))],
        out_specs=pl.BlockSpec((1, t_new, d), lambda bi, p, pt: (bi, 0, 0)),
        scratch_shapes=[pltpu.VMEM((rows, d), BF16), pltpu.VMEM((page, d), F32), pltpu.VMEM((page, d), F32),
                        pltpu.VMEM((rows, 1), F32), pltpu.VMEM((rows, 1), F32), pltpu.VMEM((rows, d), F32)],
    )
    return pl.pallas_call(
        functools.partial(_attn_sample_kernel, lam_init=lam_init),
        out_shape=jax.ShapeDtypeStruct((b, t_new, d), F32),
        grid_spec=grid_spec,
        compiler_params=_params("arbitrary", "arbitrary"),
        name="attn_sample",
    )(page_table.reshape(-1), q, k_new, v_new, cache_k, cache_v, lam_vecs, g_sub_tiled)


def _row(v):
    return v.reshape(1, -1).astype(F32)


def _post_mixer(h, p, i, W, tm):
    h = _ffn(h, _row(W['g_ffn_pre'][i]), W['w_ffn_gate'][i], W['w_ffn_up'][i], W['w_ffn_down'][i],
             _row(W['g_ffn_post'][i]), tm=tm, tf=512)
    return _ple(h, p[i], W['w_ple_gate'][i], W['w_ple_proj'][i], _row(W['g_ple_post'][i]), tm=tm)


def _conv_layer(x, conv_fn, W, tm):
    glu = _pw1_glu(x, _row(W['g_mix_pre'][0]), W['w_pw1'][0], _row(W['b_pw1'][0]), tm=tm, tn=512)
    act, extra = conv_fn(glu)
    h = _mm_norm_res(act, W['w_pw2'][0], _row(W['b_pw2'][0]), _row(W['g_mix_post'][0]), x, tm=tm)
    return h, extra


def kernel(x_prompt, x_sample, p_prompt, p_sample, state_conv, cache_k, cache_v, page_table, g_mix_pre, g_mix_post, g_ffn_pre, g_ffn_post, g_ple_post, w_pw1, b_pw1, w_dw, b_dw, g_conv_ln, b_conv_ln, w_pw2, b_pw2, g_kv_norm, w_k, w_v, w_q, lam_q1, lam_k1, lam_q2, lam_k2, g_subln, w_o, w_ffn_gate, w_ffn_up, w_ffn_down, w_ple_proj, w_ple_gate):
    B, S, D = x_prompt.shape
    DB, T, _ = x_sample.shape
    heads = D // HEAD_W
    n_pool, page = cache_k.shape[0], cache_k.shape[1]
    assert state_conv.shape[0] == 1 and w_q.shape[0] == 1, "one conv layer then one attention layer"

    W = {
        'g_mix_pre': g_mix_pre, 'g_mix_post': g_mix_post, 'g_ffn_pre': g_ffn_pre,
        'g_ffn_post': g_ffn_post, 'g_ple_post': g_ple_post, 'b_pw1': b_pw1, 'b_pw2': b_pw2,
        'w_pw1': w_pw1.astype(BF16), 'w_pw2': w_pw2.astype(BF16),
        'w_ffn_gate': w_ffn_gate.astype(BF16), 'w_ffn_up': w_ffn_up.astype(BF16),
        'w_ffn_down': w_ffn_down.astype(BF16), 'w_ple_proj': w_ple_proj.astype(BF16),
        'w_ple_gate': w_ple_gate.astype(BF16),
    }
    wq, wk, wv, wo = w_q[0].astype(BF16), w_k.astype(BF16), w_v.astype(BF16), w_o[0].astype(BF16)
    w_dw0, b_dw0 = w_dw[0], _row(b_dw[0])
    g_ln0, b_ln0 = _row(g_conv_ln[0]), _row(b_conv_ln[0])
    lam_vecs = jnp.concatenate([lam_q1, lam_k1, lam_q2, lam_k2], axis=0).astype(F32)
    lam_init = 0.8 - 0.6 * math.exp(-0.3 * 1)
    g_sub = _row(g_subln[0])

    def attention_layer(h, pos, attn_fn, tm, q_dtype):
        cos, sin_signed = _rope_tables(pos)
        q, k, v = _qkv(h, _row(g_mix_pre[1]), _row(g_kv_norm), wq, wk, wv, cos, sin_signed,
                       tm=tm, tn=512, q_dtype=q_dtype)
        o = attn_fn(q, k, v)
        h = _mm_norm_res(o, wo, None, _row(g_mix_post[1]), h, tm=tm)
        return h, k, v

    tm_p = 512
    xp = x_prompt.reshape(B * S, D)
    pp = p_prompt.reshape(p_prompt.shape[0], B * S, -1)

    def conv_p(glu):
        act = _conv_prompt(glu, w_dw0, b_dw0, g_ln0, b_ln0, seq=S, tm=tm_p)
        return act, glu.reshape(B, S, D)[:, S - CONV_HIST:, :][None]

    h, conv_state_p = _conv_layer(xp, conv_p, W, tm_p)
    h = _post_mixer(h, pp, 0, W, tm_p)
    h, k_p, v_p = attention_layer(
        h, jnp.tile(jnp.arange(S), B),
        lambda q, k, v: _attn_prompt(q, k, v, lam_vecs, g_sub, batch=B, seq=S, tq=256, tk=256,
                                     lam_init=lam_init),
        tm_p, BF16)
    y_p = _post_mixer(h, pp, 1, W, tm_p).reshape(B, S, D)

    tm_s = 512
    xs = x_sample.reshape(DB * T, D)
    ps = p_sample.reshape(p_sample.shape[0], DB * T, -1)
    past_len = page_table.shape[1] * page

    def conv_s(glu):
        return _conv_sample(state_conv[0], glu, w_dw0, b_dw0, g_ln0, b_ln0, nb=8)

    def attn_s(q, k, v):
        o = _attn_sample(q.reshape(DB, T, D), k.reshape(DB, T, D), v.reshape(DB, T, D),
                         cache_k.reshape(n_pool, page, D), cache_v.reshape(n_pool, page, D),
                         page_table, lam_vecs, jnp.tile(g_sub, (1, heads)), lam_init=lam_init)
        return o.reshape(DB * T, D)

    h, conv_state_s = _conv_layer(xs, conv_s, W, tm_s)
    h = _post_mixer(h, ps, 0, W, tm_s)
    h, k_s, v_s = attention_layer(h, jnp.tile(past_len + jnp.arange(T), DB), attn_s, tm_s, F32)
    y_s = _post_mixer(h, ps, 1, W, tm_s).reshape(DB, T, D)

    return (y_p, y_s, conv_state_p,
            k_p.reshape(B, S, heads, HEAD_W), v_p.reshape(B, S, heads, HEAD_W),
            conv_state_s[None],
            k_s.reshape(DB, T, heads, HEAD_W), v_s.reshape(DB, T, heads, HEAD_W))
```

```python
import functools
import math

import jax
import jax.numpy as jnp
from jax import lax
from jax.experimental import pallas as pl
from jax.experimental.pallas import tpu as pltpu

F32 = jnp.float32
BF16 = jnp.bfloat16

RMS_EPS = 1e-6
LN_EPS = 1e-5
ROPE_THETA = 10000.0
HEAD_DIM = 64
HEAD_W = 2 * HEAD_DIM
ROPE_HALF = HEAD_DIM // 2
CONV_W = 31
CONV_HIST = CONV_W - 1
CONV_HALO = 32
SUBLANES = 8
LANES = 128
V7X_VMEM_LIMIT = 56 * 1024 * 1024


def _params(*sem):
    return pltpu.CompilerParams(dimension_semantics=sem, vmem_limit_bytes=V7X_VMEM_LIMIT)


def _sigmoid(x):
    return 1.0 / (1.0 + jnp.exp(-x))


def _rms_scale(x):
    return x * lax.rsqrt(jnp.mean(x * x, axis=-1, keepdims=True) + RMS_EPS)


def _tile(m, want):
    t = min(m, want)
    assert m % t == 0, (m, t)
    return t


def _pw1_glu_kernel(x_ref, g_ref, wa_ref, wg_ref, ba_ref, bg_ref, o_ref, u_ref):
    @pl.when(pl.program_id(1) == 0)
    def _():
        u_ref[...] = (_rms_scale(x_ref[...]) * g_ref[...]).astype(BF16)

    u = u_ref[...]
    a = jnp.dot(u, wa_ref[...], preferred_element_type=F32) + ba_ref[...]
    gt = jnp.dot(u, wg_ref[...], preferred_element_type=F32) + bg_ref[...]
    o_ref[...] = a * _sigmoid(gt)


def _pw1_glu(x, g, w, b, *, tm, tn):
    m, d = x.shape
    tm, tn = _tile(m, tm), _tile(d, tn)
    nj = d // tn
    return pl.pallas_call(
        _pw1_glu_kernel,
        out_shape=jax.ShapeDtypeStruct((m, d), F32),
        grid=(m // tm, nj),
        in_specs=[
            pl.BlockSpec((tm, d), lambda i, j: (i, 0)),
            pl.BlockSpec((1, d), lambda i, j: (0, 0)),
            pl.BlockSpec((d, tn), lambda i, j: (0, j)),
            pl.BlockSpec((d, tn), lambda i, j: (0, j + nj)),
            pl.BlockSpec((1, tn), lambda i, j: (0, j)),
            pl.BlockSpec((1, tn), lambda i, j: (0, j + nj)),
        ],
        out_specs=pl.BlockSpec((tm, tn), lambda i, j: (i, j)),
        scratch_shapes=[pltpu.VMEM((tm, d), BF16)],
        compiler_params=_params("arbitrary", "arbitrary"),
        name="pw1_glu",
    )(x, g, w, w, b, b)


def _ln_silu(c, g, b):
    mu = jnp.mean(c, axis=-1, keepdims=True)
    xc = c - mu
    y = xc * lax.rsqrt(jnp.mean(xc * xc, axis=-1, keepdims=True) + LN_EPS) * g + b
    return y * _sigmoid(y)


CONV_ROWS_PER_ITER = 4 * SUBLANES


def _conv_prompt_kernel(cur_ref, prev_ref, w_ref, bdw_ref, gln_ref, bln_ref, o_ref, buf_ref, c_ref,
                        *, tiles_per_seq):
    tm, d = cur_ref.shape
    first = (pl.program_id(0) % tiles_per_seq) == 0
    lead = CONV_HALO - CONV_HIST
    groups = CONV_ROWS_PER_ITER // SUBLANES

    for lb in range(d // LANES):
        lanes = slice(lb * LANES, (lb + 1) * LANES)
        buf_ref[lb, 0:CONV_HALO, :] = jnp.where(first, 0.0, prev_ref[:, lanes])
        buf_ref[lb, CONV_HALO:, :] = cur_ref[:, lanes]
        taps = [jnp.broadcast_to(w_ref[j:j + 1, lanes], (SUBLANES, LANES)) for j in range(CONV_W)]
        bias = jnp.broadcast_to(bdw_ref[:, lanes], (SUBLANES, LANES))

        def body(r, carry, lb=lb, taps=taps, bias=bias):
            r0 = pl.multiple_of(r * CONV_ROWS_PER_ITER, CONV_ROWS_PER_ITER)
            for q in range(groups):
                base = r0 + (q // 2) * 2 * SUBLANES + (q % 2)
                acc = bias
                for j in range(CONV_W):
                    acc = acc + taps[j] * buf_ref[lb, pl.ds(base + lead + j, SUBLANES, stride=2), :]
                c_ref[lb, pl.ds(base, SUBLANES, stride=2), :] = acc
            return carry

        lax.fori_loop(0, tm // CONV_ROWS_PER_ITER, body, 0)

    c = c_ref[...]
    mu = jnp.sum(jnp.sum(c, axis=0), axis=-1, keepdims=True) / d
    xc = c - mu[None]
    var = jnp.sum(jnp.sum(xc * xc, axis=0), axis=-1, keepdims=True) / d
    y = xc * lax.rsqrt(var + LN_EPS)[None] * gln_ref[...] + bln_ref[...]
    act = y * _sigmoid(y)
    for lb in range(d // LANES):
        o_ref[:, lb * LANES:(lb + 1) * LANES] = act[lb].astype(BF16)


def _conv_prompt(glu, w_dw, b_dw, g_ln, b_ln, *, seq, tm):
    m, d = glu.shape
    tm = _tile(seq, tm)
    halo_blocks = tm // CONV_HALO
    nlb = d // LANES
    chan = pl.BlockSpec((nlb, 1, LANES), lambda i: (0, 0, 0))
    return pl.pallas_call(
        functools.partial(_conv_prompt_kernel, tiles_per_seq=seq // tm),
        out_shape=jax.ShapeDtypeStruct((m, d), BF16),
        grid=(m // tm,),
        in_specs=[
            pl.BlockSpec((tm, d), lambda i: (i, 0)),
            pl.BlockSpec((CONV_HALO, d), lambda i: (jnp.maximum(i * halo_blocks - 1, 0), 0)),
            pl.BlockSpec((CONV_W, d), lambda i: (0, 0)),
            pl.BlockSpec((1, d), lambda i: (0, 0)),
            chan,
            chan,
        ],
        out_specs=pl.BlockSpec((tm, d), lambda i: (i, 0)),
        scratch_shapes=[pltpu.VMEM((nlb, CONV_HALO + tm, LANES), F32), pltpu.VMEM((nlb, tm, LANES), F32)],
        compiler_params=_params("arbitrary"),
        name="conv_prompt",
    )(glu, glu, w_dw, b_dw, g_ln.reshape(nlb, 1, LANES), b_ln.reshape(nlb, 1, LANES))


def _conv_sample_kernel(st_ref, glu_ref, w_ref, bdw_ref, gln_ref, bln_ref, o_ref, ns_ref, pad_ref, c_ref,
                        *, t_new):
    nb = st_ref.shape[0]
    w = w_ref[...]
    for bi in range(nb):
        new = glu_ref[bi * t_new:(bi + 1) * t_new, :]
        pad_ref[0:CONV_HIST, :] = st_ref[bi]
        pad_ref[CONV_HIST:CONV_HIST + t_new, :] = new
        for t in range(t_new):
            c_ref[bi * t_new + t:bi * t_new + t + 1, :] = (
                jnp.sum(pad_ref[t:t + CONV_W, :] * w, axis=0, keepdims=True) + bdw_ref[...])
        ns_ref[bi] = pad_ref[t_new:t_new + CONV_HIST, :]
    o_ref[...] = _ln_silu(c_ref[...], gln_ref[...], bln_ref[...]).astype(BF16)


def _conv_sample(state, glu, w_dw, b_dw, g_ln, b_ln, *, nb):
    b, hist, d = state.shape
    t_new = glu.shape[0] // b
    nb = _tile(b, nb)
    rows = nb * t_new
    return pl.pallas_call(
        functools.partial(_conv_sample_kernel, t_new=t_new),
        out_shape=(jax.ShapeDtypeStruct((b * t_new, d), BF16), jax.ShapeDtypeStruct((b, hist, d), F32)),
        grid=(b // nb,),
        in_specs=[
            pl.BlockSpec((nb, hist, d), lambda i: (i, 0, 0)),
            pl.BlockSpec((rows, d), lambda i: (i, 0)),
            pl.BlockSpec((CONV_W, d), lambda i: (0, 0)),
            pl.BlockSpec((1, d), lambda i: (0, 0)),
            pl.BlockSpec((1, d), lambda i: (0, 0)),
            pl.BlockSpec((1, d), lambda i: (0, 0)),
        ],
        out_specs=(pl.BlockSpec((rows, d), lambda i: (i, 0)),
                   pl.BlockSpec((nb, hist, d), lambda i: (i, 0, 0))),
        scratch_shapes=[pltpu.VMEM((CONV_HIST + t_new + SUBLANES, d), F32), pltpu.VMEM((rows, d), F32)],
        compiler_params=_params("arbitrary"),
        name="conv_sample",
    )(state, glu, w_dw, b_dw, g_ln, b_ln)


def _mm_norm_res_kernel(*refs, has_bias):
    if has_bias:
        a_ref, w_ref, b_ref, g_ref, r_ref, o_ref = refs
    else:
        a_ref, w_ref, g_ref, r_ref, o_ref = refs
    m = jnp.dot(a_ref[...].astype(BF16), w_ref[...], preferred_element_type=F32)
    if has_bias:
        m = m + b_ref[...]
    o_ref[...] = r_ref[...] + _rms_scale(m) * g_ref[...]


def _mm_norm_res(a, w, bias, g, res, *, tm):
    m, k = a.shape
    d = w.shape[1]
    tm = _tile(m, tm)
    row = pl.BlockSpec((1, d), lambda i: (0, 0))
    has_bias = bias is not None
    in_specs = [pl.BlockSpec((tm, k), lambda i: (i, 0)), pl.BlockSpec((k, d), lambda i: (0, 0))]
    args = [a, w]
    if has_bias:
        in_specs.append(row)
        args.append(bias)
    in_specs += [row, pl.BlockSpec((tm, d), lambda i: (i, 0))]
    args += [g, res]
    return pl.pallas_call(
        functools.partial(_mm_norm_res_kernel, has_bias=has_bias),
        out_shape=jax.ShapeDtypeStruct((m, d), F32),
        grid=(m // tm,),
        in_specs=in_specs,
        out_specs=pl.BlockSpec((tm, d), lambda i: (i, 0)),
        compiler_params=_params("arbitrary"),
        name="mm_norm_res",
    )(*args)


def _ffn_kernel(h_ref, gpre_ref, wg_ref, wu_ref, wd_ref, gpost_ref, o_ref, z_ref, acc_ref):
    c = pl.program_id(1)

    @pl.when(c == 0)
    def _():
        z_ref[...] = (_rms_scale(h_ref[...]) * gpre_ref[...]).astype(BF16)

    z = z_ref[...]
    gate = jnp.dot(z, wg_ref[...], preferred_element_type=F32)
    up = jnp.dot(z, wu_ref[...], preferred_element_type=F32)
    act = (gate * _sigmoid(gate) * up).astype(BF16)
    part = jnp.dot(act, wd_ref[...], preferred_element_type=F32)

    @pl.when(c == 0)
    def _():
        acc_ref[...] = part

    @pl.when(c > 0)
    def _():
        acc_ref[...] += part

    @pl.when(c == pl.num_programs(1) - 1)
    def _():
        o_ref[...] = h_ref[...] + _rms_scale(acc_ref[...]) * gpost_ref[...]


def _ffn(h, g_pre, wg, wu, wd, g_post, *, tm, tf):
    m, d = h.shape
    f = wg.shape[1]
    tm, tf = _tile(m, tm), _tile(f, tf)
    row = pl.BlockSpec((1, d), lambda i, c: (0, 0))
    return pl.pallas_call(
        _ffn_kernel,
        out_shape=jax.ShapeDtypeStruct((m, d), F32),
        grid=(m // tm, f // tf),
        in_specs=[
            pl.BlockSpec((tm, d), lambda i, c: (i, 0)),
            row,
            pl.BlockSpec((d, tf), lambda i, c: (0, c)),
            pl.BlockSpec((d, tf), lambda i, c: (0, c)),
            pl.BlockSpec((tf, d), lambda i, c: (c, 0)),
            row,
        ],
        out_specs=pl.BlockSpec((tm, d), lambda i, c: (i, 0)),
        scratch_shapes=[pltpu.VMEM((tm, d), BF16), pltpu.VMEM((tm, d), F32)],
        compiler_params=_params("arbitrary", "arbitrary"),
        name="ffn",
    )(h, g_pre, wg, wu, wd, g_post)


def _ple_kernel(h_ref, p_ref, wgate_ref, wproj_ref, g_ref, o_ref):
    h = h_ref[...]
    gate = _sigmoid(jnp.dot(h.astype(BF16), wgate_ref[...], preferred_element_type=F32))
    e = jnp.dot(p_ref[...].astype(BF16), wproj_ref[...], preferred_element_type=F32)
    o_ref[...] = h + _rms_scale(gate * e) * g_ref[...]


def _ple(h, p, w_gate, w_proj, g, *, tm):
    m, d = h.shape
    pd = p.shape[1]
    tm = _tile(m, tm)
    return pl.pallas_call(
        _ple_kernel,
        out_shape=jax.ShapeDtypeStruct((m, d), F32),
        grid=(m // tm,),
        in_specs=[
            pl.BlockSpec((tm, d), lambda i: (i, 0)),
            pl.BlockSpec((tm, pd), lambda i: (i, 0)),
            pl.BlockSpec((d, d), lambda i: (0, 0)),
            pl.BlockSpec((pd, d), lambda i: (0, 0)),
            pl.BlockSpec((1, d), lambda i: (0, 0)),
        ],
        out_specs=pl.BlockSpec((tm, d), lambda i: (i, 0)),
        compiler_params=_params("arbitrary"),
        name="ple",
    )(h, p, w_gate, w_proj, g)


def _rope(y, cos, sin_signed):
    lane = lax.broadcasted_iota(jnp.int32, (y.shape[0], LANES), 1)
    low_half = (lane % HEAD_DIM) < ROPE_HALF
    out = []
    for lb in range(y.shape[1] // LANES):
        yb = y[:, lb * LANES:(lb + 1) * LANES]
        partner = jnp.where(low_half, pltpu.roll(yb, LANES - ROPE_HALF, 1), pltpu.roll(yb, ROPE_HALF, 1))
        out.append(yb * cos + partner * sin_signed)
    return out


def _qkv_kernel(h_ref, gu_ref, gs_ref, wq_ref, wk_ref, wv_ref, cos_ref, sin_ref,
                q_ref, k_ref, v_ref, u_ref, s_ref):
    @pl.when(pl.program_id(1) == 0)
    def _():
        r = _rms_scale(h_ref[...])
        u_ref[...] = (r * gu_ref[...]).astype(BF16)
        s_ref[...] = (r * gs_ref[...]).astype(BF16)

    cos, sin_signed = cos_ref[...], sin_ref[...]
    s = s_ref[...]
    q = _rope(jnp.dot(u_ref[...], wq_ref[...], preferred_element_type=F32), cos, sin_signed)
    k = _rope(jnp.dot(s, wk_ref[...], preferred_element_type=F32), cos, sin_signed)
    for lb in range(len(q)):
        lanes = slice(lb * LANES, (lb + 1) * LANES)
        q_ref[:, lanes] = (q[lb] * (HEAD_DIM ** -0.5)).astype(q_ref.dtype)
        k_ref[:, lanes] = k[lb]
    v_ref[...] = jnp.dot(s, wv_ref[...], preferred_element_type=F32)


def _qkv(h, g_u, g_s, wq, wk, wv, cos, sin_signed, *, tm, tn, q_dtype):
    m, d = h.shape
    tm, tn = _tile(m, tm), _tile(d, tn)
    row = pl.BlockSpec((1, d), lambda i, j: (0, 0))
    wcol = pl.BlockSpec((d, tn), lambda i, j: (0, j))
    tab = pl.BlockSpec((tm, LANES), lambda i, j: (i, 0))
    out = pl.BlockSpec((tm, tn), lambda i, j: (i, j))
    return pl.pallas_call(
        _qkv_kernel,
        out_shape=(jax.ShapeDtypeStruct((m, d), q_dtype), jax.ShapeDtypeStruct((m, d), F32),
                   jax.ShapeDtypeStruct((m, d), F32)),
        grid=(m // tm, d // tn),
        in_specs=[pl.BlockSpec((tm, d), lambda i, j: (i, 0)), row, row, wcol, wcol, wcol, tab, tab],
        out_specs=(out, out, out),
        scratch_shapes=[pltpu.VMEM((tm, d), BF16), pltpu.VMEM((tm, d), BF16)],
        compiler_params=_params("arbitrary", "arbitrary"),
        name="qkv_rope",
    )(h, g_u, g_s, wq, wk, wv, cos, sin_signed)


def _rope_tables(pos):
    inv = 1.0 / (ROPE_THETA ** (jnp.arange(ROPE_HALF, dtype=F32) / ROPE_HALF))
    ang = pos.astype(F32)[:, None] * inv[None, :]
    cos, sin = jnp.cos(ang), jnp.sin(ang)
    reps = LANES // HEAD_DIM
    cos_l = jnp.tile(jnp.concatenate([cos, cos], axis=-1), (1, reps))
    sin_l = jnp.tile(jnp.concatenate([-sin, sin], axis=-1), (1, reps))
    return cos_l, sin_l


def _lambda(lam_ref, lam_init):
    lam = lam_ref[...]
    d1 = jnp.sum(lam[0:1] * lam[1:2], axis=-1, keepdims=True)
    d2 = jnp.sum(lam[2:3] * lam[3:4], axis=-1, keepdims=True)
    return jnp.exp(d1) - jnp.exp(d2) + lam_init


def _attn_prompt_kernel(q_ref, k_ref, v_ref, lam_ref, gsub_ref, o_ref, kb_ref, vb_ref, *, lam_init):
    tq = q_ref.shape[0]
    i = pl.program_id(2)
    nt = (((1,), (1,)), ((), ()))

    @pl.when(i == 0)
    def _():
        kb_ref[...] = k_ref[...].astype(BF16)
        vb_ref[:, 0:HEAD_W] = v_ref[...].astype(BF16)
        vb_ref[:, HEAD_W:] = jnp.ones((vb_ref.shape[0], HEAD_W), BF16)

    def attend(tile):
        past = tile * tq
        q = q_ref[...]
        lane = lax.broadcasted_iota(jnp.int32, q.shape, 1)
        zero = jnp.zeros_like(q)
        row = lax.broadcasted_iota(jnp.int32, (tq, tq), 0)
        col = lax.broadcasted_iota(jnp.int32, (tq, tq), 1)
        out = []
        for c in range(2):
            qc = jnp.where((lane // HEAD_DIM) == c, q, zero)
            s_diag = lax.dot_general(qc, kb_ref[past:past + tq, :], nt, preferred_element_type=F32)
            s_diag = jnp.where(col <= row, s_diag, -jnp.inf)
            m = jnp.max(s_diag, axis=-1, keepdims=True)
            if tile:
                s_past = lax.dot_general(qc, kb_ref[0:past, :], nt, preferred_element_type=F32)
                m = jnp.maximum(m, jnp.max(s_past, axis=-1, keepdims=True))
            acc = jnp.dot(jnp.exp(s_diag - m).astype(BF16), vb_ref[past:past + tq, :],
                          preferred_element_type=F32)
            if tile:
                acc = acc + jnp.dot(jnp.exp(s_past - m).astype(BF16), vb_ref[0:past, :],
                                    preferred_element_type=F32)
            out.append(acc[:, 0:HEAD_W] / acc[:, HEAD_W:])
        o = out[0] - _lambda(lam_ref, lam_init) * out[1]
        o_ref[...] = (_rms_scale(o) * gsub_ref[...] * (1.0 - lam_init)).astype(o_ref.dtype)

    for tile in range(kb_ref.shape[0] // tq):
        pl.when(i == tile)(functools.partial(attend, tile))


def _attn_prompt(q, k, v, lam_vecs, g_sub, *, batch, seq, tq, lam_init):
    m, d = q.shape
    heads = d // HEAD_W
    tq = _tile(seq, tq)
    nq = seq // tq
    return pl.pallas_call(
        functools.partial(_attn_prompt_kernel, lam_init=lam_init),
        out_shape=jax.ShapeDtypeStruct((m, d), BF16),
        grid=(batch, heads, nq),
        in_specs=[
            pl.BlockSpec((tq, HEAD_W), lambda b, h, i: (b * nq + i, h)),
            pl.BlockSpec((seq, HEAD_W), lambda b, h, i: (b, h)),
            pl.BlockSpec((seq, HEAD_W), lambda b, h, i: (b, h)),
            pl.BlockSpec(lam_vecs.shape, lambda b, h, i: (0, 0)),
            pl.BlockSpec((1, HEAD_W), lambda b, h, i: (0, 0)),
        ],
        out_specs=pl.BlockSpec((tq, HEAD_W), lambda b, h, i: (b * nq + i, h)),
        scratch_shapes=[pltpu.VMEM((seq, HEAD_W), BF16), pltpu.VMEM((seq, 2 * HEAD_W), BF16)],
        compiler_params=_params("arbitrary", "arbitrary", "arbitrary"),
        name="attn_prompt",
    )(q, k, v, lam_vecs, g_sub)


PAGES_PER_STEP = 4


def _attn_sample_kernel(*refs, lam_init, n_group):
    pt_ref, q_ref, kn_ref, vn_ref = refs[:4]
    kc_refs = refs[4:4 + n_group]
    vc_refs = refs[4 + n_group:4 + 2 * n_group]
    lam_ref, gsub_ref, o_ref, qrows_ref, bias_ref, m_ref, l_ref, acc_ref = refs[4 + 2 * n_group:]
    del pt_ref
    t_new, heads = q_ref.shape[1], q_ref.shape[2]
    rows = 2 * t_new * heads
    page = kc_refs[0].shape[1]
    step = pl.program_id(1)
    nt = (((1,), (1,)), ((), ()))

    @pl.when(step == 0)
    def _():
        lane = lax.broadcasted_iota(jnp.int32, (heads, HEAD_W), 1)
        for c in range(2):
            own = (lane // HEAD_DIM) == c
            for t in range(t_new):
                r0 = (c * t_new + t) * heads
                qrows_ref[r0:r0 + heads, :] = jnp.where(own, q_ref[0, t], 0.0).astype(BF16)
        row_head = lax.broadcasted_iota(jnp.int32, bias_ref.shape, 0) % heads
        key_head = lax.broadcasted_iota(jnp.int32, bias_ref.shape, 1) % heads
        bias_ref[...] = jnp.where(row_head == key_head, 0.0, -jnp.inf)
        m_ref[...] = jnp.full(m_ref.shape, -jnp.inf, F32)
        l_ref[...] = jnp.zeros(l_ref.shape, F32)
        acc_ref[...] = jnp.zeros(acc_ref.shape, F32)

    def update(scores, values):
        m_prev = m_ref[...]
        m_new = m_prev
        for s in scores:
            m_new = jnp.maximum(m_new, jnp.max(s, axis=-1, keepdims=True))
        alpha = jnp.exp(m_prev - m_new)
        l_new = alpha * l_ref[...]
        acc = alpha * acc_ref[...]
        for s, v in zip(scores, values):
            p = jnp.exp(s - m_new)
            l_new = l_new + jnp.sum(p, axis=-1, keepdims=True)
            acc = acc + jnp.dot(p.astype(BF16), v, preferred_element_type=F32)
        m_ref[...], l_ref[...], acc_ref[...] = m_new, l_new, acc

    q_rows = qrows_ref[...]
    bias = bias_ref[...]
    scores, values = [], []
    for kc_ref, vc_ref in zip(kc_refs, vc_refs):
        keys = kc_ref[0].reshape(page * heads, HEAD_W).astype(BF16)
        scores.append(lax.dot_general(q_rows, keys, nt, preferred_element_type=F32) + bias)
        values.append(vc_ref[0].reshape(page * heads, HEAD_W).astype(BF16))
    update(scores, values)

    @pl.when(step == pl.num_programs(1) - 1)
    def _():
        keys = kn_ref[0].reshape(t_new * heads, HEAD_W).astype(BF16)
        s = lax.dot_general(q_rows, keys, nt, preferred_element_type=F32)
        row = lax.broadcasted_iota(jnp.int32, s.shape, 0)
        col = lax.broadcasted_iota(jnp.int32, s.shape, 1)
        ok = (row % heads == col % heads) & (col // heads <= (row // heads) % t_new)
        update([jnp.where(ok, s, -jnp.inf)], [vn_ref[0].reshape(t_new * heads, HEAD_W).astype(BF16)])

        half = rows // 2
        lam = _lambda(lam_ref, lam_init)
        o = acc_ref[...] / l_ref[...]
        w = o[0:half] - lam * o[half:rows]
        w = _rms_scale(w) * gsub_ref[...] * (1.0 - lam_init)
        o_ref[0] = w.reshape(t_new, heads, HEAD_W)


def _attn_sample(q, k_new, v_new, cache_k, cache_v, page_table, lam_vecs, g_sub, *, lam_init):
    b, t_new, heads, _ = q.shape
    n_pages = page_table.shape[1]
    page = cache_k.shape[1]
    rows = 2 * t_new * heads
    n_group = math.gcd(n_pages, PAGES_PER_STEP)
    tok = pl.BlockSpec((1, t_new, heads, HEAD_W), lambda bi, s, pt: (bi, 0, 0, 0))

    def page_spec(g):
        return pl.BlockSpec((1, page, heads, HEAD_W),
                            lambda bi, s, pt: (pt[bi * n_pages + s * n_group + g], 0, 0, 0))

    pages = [page_spec(g) for g in range(n_group)]
    grid_spec = pltpu.PrefetchScalarGridSpec(
        num_scalar_prefetch=1,
        grid=(b, n_pages // n_group),
        in_specs=[tok, tok, tok, *pages, *pages,
                  pl.BlockSpec(lam_vecs.shape, lambda bi, s, pt: (0, 0)),
                  pl.BlockSpec((1, HEAD_W), lambda bi, s, pt: (0, 0))],
        out_specs=tok,
        scratch_shapes=[pltpu.VMEM((rows, HEAD_W), BF16), pltpu.VMEM((rows, page * heads), F32),
                        pltpu.VMEM((rows, 1), F32), pltpu.VMEM((rows, 1), F32),
                        pltpu.VMEM((rows, HEAD_W), F32)],
    )
    return pl.pallas_call(
        functools.partial(_attn_sample_kernel, lam_init=lam_init, n_group=n_group),
        out_shape=jax.ShapeDtypeStruct((b, t_new, heads, HEAD_W), F32),
        grid_spec=grid_spec,
        compiler_params=_params("arbitrary", "arbitrary"),
        name="attn_sample",
    )(page_table.reshape(-1), q, k_new, v_new, *([cache_k] * n_group), *([cache_v] * n_group),
      lam_vecs, g_sub)


def _row(v):
    return v.reshape(1, -1).astype(F32)


def _post_mixer(h, p, i, W, tm):
    h = _ffn(h, _row(W['g_ffn_pre'][i]), W['w_ffn_gate'][i], W['w_ffn_up'][i], W['w_ffn_down'][i],
             _row(W['g_ffn_post'][i]), tm=tm, tf=512)
    return _ple(h, p[i], W['w_ple_gate'][i], W['w_ple_proj'][i], _row(W['g_ple_post'][i]), tm=tm)


def _conv_layer(x, conv_fn, W, tm):
    glu = _pw1_glu(x, _row(W['g_mix_pre'][0]), W['w_pw1'][0], _row(W['b_pw1'][0]), tm=tm, tn=512)
    act, extra = conv_fn(glu)
    h = _mm_norm_res(act, W['w_pw2'][0], _row(W['b_pw2'][0]), _row(W['g_mix_post'][0]), x, tm=tm)
    return h, extra


def kernel(x_prompt, x_sample, p_prompt, p_sample, state_conv, cache_k, cache_v, page_table, g_mix_pre, g_mix_post, g_ffn_pre, g_ffn_post, g_ple_post, w_pw1, b_pw1, w_dw, b_dw, g_conv_ln, b_conv_ln, w_pw2, b_pw2, g_kv_norm, w_k, w_v, w_q, lam_q1, lam_k1, lam_q2, lam_k2, g_subln, w_o, w_ffn_gate, w_ffn_up, w_ffn_down, w_ple_proj, w_ple_gate):
    B, S, D = x_prompt.shape
    DB, T, _ = x_sample.shape
    heads = D // HEAD_W
    n_pool, page = cache_k.shape[0], cache_k.shape[1]
    assert state_conv.shape[0] == 1 and w_q.shape[0] == 1, "one conv layer then one attention layer"

    W = {
        'g_mix_pre': g_mix_pre, 'g_mix_post': g_mix_post, 'g_ffn_pre': g_ffn_pre,
        'g_ffn_post': g_ffn_post, 'g_ple_post': g_ple_post, 'b_pw1': b_pw1, 'b_pw2': b_pw2,
        'w_pw1': w_pw1.astype(BF16), 'w_pw2': w_pw2.astype(BF16),
        'w_ffn_gate': w_ffn_gate.astype(BF16), 'w_ffn_up': w_ffn_up.astype(BF16),
        'w_ffn_down': w_ffn_down.astype(BF16), 'w_ple_proj': w_ple_proj.astype(BF16),
        'w_ple_gate': w_ple_gate.astype(BF16),
    }
    wq, wk, wv, wo = w_q[0].astype(BF16), w_k.astype(BF16), w_v.astype(BF16), w_o[0].astype(BF16)
    w_dw0, b_dw0 = w_dw[0], _row(b_dw[0])
    g_ln0, b_ln0 = _row(g_conv_ln[0]), _row(b_conv_ln[0])
    lam_vecs = jnp.concatenate([lam_q1, lam_k1, lam_q2, lam_k2], axis=0).astype(F32)
    lam_init = 0.8 - 0.6 * math.exp(-0.3 * 1)
    g_sub = _row(g_subln[0])

    def attention_layer(h, pos, attn_fn, tm, q_dtype):
        cos, sin_signed = _rope_tables(pos)
        q, k, v = _qkv(h, _row(g_mix_pre[1]), _row(g_kv_norm), wq, wk, wv, cos, sin_signed,
                       tm=tm, tn=512, q_dtype=q_dtype)
        o = attn_fn(q, k, v)
        h = _mm_norm_res(o, wo, None, _row(g_mix_post[1]), h, tm=tm)
        return h, k, v

    tm_p = 512
    xp = x_prompt.reshape(B * S, D)
    pp = p_prompt.reshape(p_prompt.shape[0], B * S, -1)

    def conv_p(glu):
        act = _conv_prompt(glu, w_dw0, b_dw0, g_ln0, b_ln0, seq=S, tm=tm_p)
        return act, glu.reshape(B, S, D)[:, S - CONV_HIST:, :][None]

    h, conv_state_p = _conv_layer(xp, conv_p, W, tm_p)
    h = _post_mixer(h, pp, 0, W, tm_p)
    h, k_p, v_p = attention_layer(
        h, jnp.tile(jnp.arange(S), B),
        lambda q, k, v: _attn_prompt(q, k, v, lam_vecs, g_sub, batch=B, seq=S, tq=512, lam_init=lam_init),
        tm_p, BF16)
    y_p = _post_mixer(h, pp, 1, W, tm_p).reshape(B, S, D)

    tm_s = 512
    xs = x_sample.reshape(DB * T, D)
    ps = p_sample.reshape(p_sample.shape[0], DB * T, -1)
    past_len = page_table.shape[1] * page

    def conv_s(glu):
        return _conv_sample(state_conv[0], glu, w_dw0, b_dw0, g_ln0, b_ln0, nb=8)

    def attn_s(q, k, v):
        per_head = (DB, T, heads, HEAD_W)
        o = _attn_sample(q.reshape(per_head), k.reshape(per_head), v.reshape(per_head),
                         cache_k, cache_v, page_table, lam_vecs, g_sub, lam_init=lam_init)
        return o.reshape(DB * T, D)

    h, conv_state_s = _conv_layer(xs, conv_s, W, tm_s)
    h = _post_mixer(h, ps, 0, W, tm_s)
    h, k_s, v_s = attention_layer(h, jnp.tile(past_len + jnp.arange(T), DB), attn_s, tm_s, F32)
    y_s = _post_mixer(h, ps, 1, W, tm_s).reshape(DB, T, D)

    return (y_p, y_s, conv_state_p,
            k_p.reshape(B, S, heads, HEAD_W), v_p.reshape(B, S, heads, HEAD_W),
            conv_state_s[None],
            k_s.reshape(DB, T, heads, HEAD_W), v_s.reshape(DB, T, heads, HEAD_W))
```

```python
import functools
import math

import jax
import jax.numpy as jnp
from jax import lax
from jax.experimental import pallas as pl
from jax.experimental.pallas import tpu as pltpu

F32 = jnp.float32
BF16 = jnp.bfloat16

RMS_EPS = 1e-6
LN_EPS = 1e-5
ROPE_THETA = 10000.0
HEAD_DIM = 64
HEAD_W = 2 * HEAD_DIM
ROPE_HALF = HEAD_DIM // 2
SCORE_SCALE = HEAD_DIM ** -0.5 * math.log2(math.e)
CONV_W = 31
CONV_HIST = CONV_W - 1
CONV_HALO = 32
SUBLANES = 8
LANES = 128
V7X_VMEM_LIMIT = 56 * 1024 * 1024


def _params(*sem):
    return pltpu.CompilerParams(dimension_semantics=sem, vmem_limit_bytes=V7X_VMEM_LIMIT)


def _sigmoid(x):
    return 1.0 / (1.0 + jnp.exp(-x))


def _rms_scale(x):
    return x * lax.rsqrt(jnp.mean(x * x, axis=-1, keepdims=True) + RMS_EPS)


def _tile(m, want):
    t = min(m, want)
    assert m % t == 0, (m, t)
    return t


def _pw1_glu_kernel(x_ref, g_ref, wa_ref, wg_ref, ba_ref, bg_ref, o_ref, u_ref):
    @pl.when(pl.program_id(1) == 0)
    def _():
        u_ref[...] = (_rms_scale(x_ref[...]) * g_ref[...]).astype(BF16)

    u = u_ref[...]
    a = jnp.dot(u, wa_ref[...], preferred_element_type=F32) + ba_ref[...]
    gt = jnp.dot(u, wg_ref[...], preferred_element_type=F32) + bg_ref[...]
    o_ref[...] = a * _sigmoid(gt)


def _pw1_glu(x, g, w, b, *, tm, tn):
    m, d = x.shape
    tm, tn = _tile(m, tm), _tile(d, tn)
    nj = d // tn
    return pl.pallas_call(
        _pw1_glu_kernel,
        out_shape=jax.ShapeDtypeStruct((m, d), F32),
        grid=(m // tm, nj),
        in_specs=[
            pl.BlockSpec((tm, d), lambda i, j: (i, 0)),
            pl.BlockSpec((1, d), lambda i, j: (0, 0)),
            pl.BlockSpec((d, tn), lambda i, j: (0, j)),
            pl.BlockSpec((d, tn), lambda i, j: (0, j + nj)),
            pl.BlockSpec((1, tn), lambda i, j: (0, j)),
            pl.BlockSpec((1, tn), lambda i, j: (0, j + nj)),
        ],
        out_specs=pl.BlockSpec((tm, tn), lambda i, j: (i, j)),
        scratch_shapes=[pltpu.VMEM((tm, d), BF16)],
        compiler_params=_params("arbitrary", "arbitrary"),
        name="pw1_glu",
    )(x, g, w, w, b, b)


def _ln_silu(c, g, b):
    mu = jnp.mean(c, axis=-1, keepdims=True)
    xc = c - mu
    y = xc * lax.rsqrt(jnp.mean(xc * xc, axis=-1, keepdims=True) + LN_EPS) * g + b
    return y * _sigmoid(y)


CONV_ROWS_PER_ITER = 4 * SUBLANES


def _conv_prompt_kernel(cur_ref, prev_ref, w_ref, bdw_ref, gln_ref, bln_ref, o_ref, buf_ref, c_ref,
                        *, tiles_per_seq):
    tm, d = cur_ref.shape
    first = (pl.program_id(0) % tiles_per_seq) == 0
    lead = CONV_HALO - CONV_HIST
    groups = CONV_ROWS_PER_ITER // SUBLANES

    for lb in range(d // LANES):
        lanes = slice(lb * LANES, (lb + 1) * LANES)
        buf_ref[lb, 0:CONV_HALO, :] = jnp.where(first, 0.0, prev_ref[:, lanes])
        buf_ref[lb, CONV_HALO:, :] = cur_ref[:, lanes]
        taps = [jnp.broadcast_to(w_ref[j:j + 1, lanes], (SUBLANES, LANES)) for j in range(CONV_W)]
        bias = jnp.broadcast_to(bdw_ref[:, lanes], (SUBLANES, LANES))

        def body(r, carry, lb=lb, taps=taps, bias=bias):
            r0 = pl.multiple_of(r * CONV_ROWS_PER_ITER, CONV_ROWS_PER_ITER)
            for q in range(groups):
                base = r0 + (q // 2) * 2 * SUBLANES + (q % 2)
                acc = bias
                for j in range(CONV_W):
                    acc = acc + taps[j] * buf_ref[lb, pl.ds(base + lead + j, SUBLANES, stride=2), :]
                c_ref[lb, pl.ds(base, SUBLANES, stride=2), :] = acc
            return carry

        lax.fori_loop(0, tm // CONV_ROWS_PER_ITER, body, 0)

    c = c_ref[...]
    mu = jnp.sum(jnp.sum(c, axis=0), axis=-1, keepdims=True) / d
    xc = c - mu[None]
    var = jnp.sum(jnp.sum(xc * xc, axis=0), axis=-1, keepdims=True) / d
    y = xc * lax.rsqrt(var + LN_EPS)[None] * gln_ref[...] + bln_ref[...]
    act = y * _sigmoid(y)
    for lb in range(d // LANES):
        o_ref[:, lb * LANES:(lb + 1) * LANES] = act[lb].astype(BF16)


def _conv_prompt(glu, w_dw, b_dw, g_ln, b_ln, *, seq, tm):
    m, d = glu.shape
    tm = _tile(seq, tm)
    halo_blocks = tm // CONV_HALO
    nlb = d // LANES
    chan = pl.BlockSpec((nlb, 1, LANES), lambda i: (0, 0, 0))
    return pl.pallas_call(
        functools.partial(_conv_prompt_kernel, tiles_per_seq=seq // tm),
        out_shape=jax.ShapeDtypeStruct((m, d), BF16),
        grid=(m // tm,),
        in_specs=[
            pl.BlockSpec((tm, d), lambda i: (i, 0)),
            pl.BlockSpec((CONV_HALO, d), lambda i: (jnp.maximum(i * halo_blocks - 1, 0), 0)),
            pl.BlockSpec((CONV_W, d), lambda i: (0, 0)),
            pl.BlockSpec((1, d), lambda i: (0, 0)),
            chan,
            chan,
        ],
        out_specs=pl.BlockSpec((tm, d), lambda i: (i, 0)),
        scratch_shapes=[pltpu.VMEM((nlb, CONV_HALO + tm, LANES), F32), pltpu.VMEM((nlb, tm, LANES), F32)],
        compiler_params=_params("arbitrary"),
        name="conv_prompt",
    )(glu, glu, w_dw, b_dw, g_ln.reshape(nlb, 1, LANES), b_ln.reshape(nlb, 1, LANES))


def _conv_sample_kernel(st_ref, glu_ref, w_ref, bdw_ref, gln_ref, bln_ref, o_ref, ns_ref, pad_ref, c_ref,
                        *, t_new):
    nb = st_ref.shape[0]
    w = w_ref[...]
    for bi in range(nb):
        new = glu_ref[bi * t_new:(bi + 1) * t_new, :]
        pad_ref[0:CONV_HIST, :] = st_ref[bi]
        pad_ref[CONV_HIST:CONV_HIST + t_new, :] = new
        for t in range(t_new):
            c_ref[bi * t_new + t:bi * t_new + t + 1, :] = (
                jnp.sum(pad_ref[t:t + CONV_W, :] * w, axis=0, keepdims=True) + bdw_ref[...])
        ns_ref[bi] = pad_ref[t_new:t_new + CONV_HIST, :]
    o_ref[...] = _ln_silu(c_ref[...], gln_ref[...], bln_ref[...]).astype(BF16)


def _conv_sample(state, glu, w_dw, b_dw, g_ln, b_ln, *, nb):
    b, hist, d = state.shape
    t_new = glu.shape[0] // b
    nb = _tile(b, nb)
    rows = nb * t_new
    return pl.pallas_call(
        functools.partial(_conv_sample_kernel, t_new=t_new),
        out_shape=(jax.ShapeDtypeStruct((b * t_new, d), BF16), jax.ShapeDtypeStruct((b, hist, d), F32)),
        grid=(b // nb,),
        in_specs=[
            pl.BlockSpec((nb, hist, d), lambda i: (i, 0, 0)),
            pl.BlockSpec((rows, d), lambda i: (i, 0)),
            pl.BlockSpec((CONV_W, d), lambda i: (0, 0)),
            pl.BlockSpec((1, d), lambda i: (0, 0)),
            pl.BlockSpec((1, d), lambda i: (0, 0)),
            pl.BlockSpec((1, d), lambda i: (0, 0)),
        ],
        out_specs=(pl.BlockSpec((rows, d), lambda i: (i, 0)),
                   pl.BlockSpec((nb, hist, d), lambda i: (i, 0, 0))),
        scratch_shapes=[pltpu.VMEM((CONV_HIST + t_new + SUBLANES, d), F32), pltpu.VMEM((rows, d), F32)],
        compiler_params=_params("arbitrary"),
        name="conv_sample",
    )(state, glu, w_dw, b_dw, g_ln, b_ln)


def _mm_norm_res_kernel(*refs, has_bias):
    if has_bias:
        a_ref, w_ref, b_ref, g_ref, r_ref, o_ref = refs
    else:
        a_ref, w_ref, g_ref, r_ref, o_ref = refs
    m = jnp.dot(a_ref[...].astype(BF16), w_ref[...], preferred_element_type=F32)
    if has_bias:
        m = m + b_ref[...]
    o_ref[...] = r_ref[...] + _rms_scale(m) * g_ref[...]


def _mm_norm_res(a, w, bias, g, res, *, tm):
    m, k = a.shape
    d = w.shape[1]
    tm = _tile(m, tm)
    row = pl.BlockSpec((1, d), lambda i: (0, 0))
    has_bias = bias is not None
    in_specs = [pl.BlockSpec((tm, k), lambda i: (i, 0)), pl.BlockSpec((k, d), lambda i: (0, 0))]
    args = [a, w]
    if has_bias:
        in_specs.append(row)
        args.append(bias)
    in_specs += [row, pl.BlockSpec((tm, d), lambda i: (i, 0))]
    args += [g, res]
    return pl.pallas_call(
        functools.partial(_mm_norm_res_kernel, has_bias=has_bias),
        out_shape=jax.ShapeDtypeStruct((m, d), F32),
        grid=(m // tm,),
        in_specs=in_specs,
        out_specs=pl.BlockSpec((tm, d), lambda i: (i, 0)),
        compiler_params=_params("arbitrary"),
        name="mm_norm_res",
    )(*args)


def _ffn_kernel(h_ref, gpre_ref, wg_ref, wu_ref, wd_ref, gpost_ref, o_ref, z_ref, acc_ref):
    c = pl.program_id(1)

    @pl.when(c == 0)
    def _():
        z_ref[...] = (_rms_scale(h_ref[...]) * gpre_ref[...]).astype(BF16)

    z = z_ref[...]
    gate = jnp.dot(z, wg_ref[...], preferred_element_type=F32)
    up = jnp.dot(z, wu_ref[...], preferred_element_type=F32)
    act = (gate * _sigmoid(gate) * up).astype(BF16)
    part = jnp.dot(act, wd_ref[...], preferred_element_type=F32)

    @pl.when(c == 0)
    def _():
        acc_ref[...] = part

    @pl.when(c > 0)
    def _():
        acc_ref[...] += part

    @pl.when(c == pl.num_programs(1) - 1)
    def _():
        o_ref[...] = h_ref[...] + _rms_scale(acc_ref[...]) * gpost_ref[...]


def _ffn(h, g_pre, wg, wu, wd, g_post, *, tm, tf):
    m, d = h.shape
    f = wg.shape[1]
    tm, tf = _tile(m, tm), _tile(f, tf)
    row = pl.BlockSpec((1, d), lambda i, c: (0, 0))
    return pl.pallas_call(
        _ffn_kernel,
        out_shape=jax.ShapeDtypeStruct((m, d), F32),
        grid=(m // tm, f // tf),
        in_specs=[
            pl.BlockSpec((tm, d), lambda i, c: (i, 0)),
            row,
            pl.BlockSpec((d, tf), lambda i, c: (0, c)),
            pl.BlockSpec((d, tf), lambda i, c: (0, c)),
            pl.BlockSpec((tf, d), lambda i, c: (c, 0)),
            row,
        ],
        out_specs=pl.BlockSpec((tm, d), lambda i, c: (i, 0)),
        scratch_shapes=[pltpu.VMEM((tm, d), BF16), pltpu.VMEM((tm, d), F32)],
        compiler_params=_params("arbitrary", "arbitrary"),
        name="ffn",
    )(h, g_pre, wg, wu, wd, g_post)


def _ple_kernel(h_ref, p_ref, wgate_ref, wproj_ref, g_ref, o_ref):
    h = h_ref[...]
    gate = _sigmoid(jnp.dot(h.astype(BF16), wgate_ref[...], preferred_element_type=F32))
    e = jnp.dot(p_ref[...].astype(BF16), wproj_ref[...], preferred_element_type=F32)
    o_ref[...] = h + _rms_scale(gate * e) * g_ref[...]


def _ple(h, p, w_gate, w_proj, g, *, tm):
    m, d = h.shape
    pd = p.shape[1]
    tm = _tile(m, tm)
    return pl.pallas_call(
        _ple_kernel,
        out_shape=jax.ShapeDtypeStruct((m, d), F32),
        grid=(m // tm,),
        in_specs=[
            pl.BlockSpec((tm, d), lambda i: (i, 0)),
            pl.BlockSpec((tm, pd), lambda i: (i, 0)),
            pl.BlockSpec((d, d), lambda i: (0, 0)),
            pl.BlockSpec((pd, d), lambda i: (0, 0)),
            pl.BlockSpec((1, d), lambda i: (0, 0)),
        ],
        out_specs=pl.BlockSpec((tm, d), lambda i: (i, 0)),
        compiler_params=_params("arbitrary"),
        name="ple",
    )(h, p, w_gate, w_proj, g)


def _rope(y, cos, sin_signed):
    lane = lax.broadcasted_iota(jnp.int32, (y.shape[0], LANES), 1)
    low_half = (lane % HEAD_DIM) < ROPE_HALF
    out = []
    for lb in range(y.shape[1] // LANES):
        yb = y[:, lb * LANES:(lb + 1) * LANES]
        partner = jnp.where(low_half, pltpu.roll(yb, LANES - ROPE_HALF, 1), pltpu.roll(yb, ROPE_HALF, 1))
        out.append(yb * cos + partner * sin_signed)
    return out


def _qkv_kernel(h_ref, gu_ref, gs_ref, wq_ref, wk_ref, wv_ref, cos_ref, sin_ref,
                q_ref, k_ref, v_ref, u_ref, s_ref):
    @pl.when(pl.program_id(1) == 0)
    def _():
        r = _rms_scale(h_ref[...])
        u_ref[...] = (r * gu_ref[...]).astype(BF16)
        s_ref[...] = (r * gs_ref[...]).astype(BF16)

    cos, sin_signed = cos_ref[...], sin_ref[...]
    s = s_ref[...]
    q = _rope(jnp.dot(u_ref[...], wq_ref[...], preferred_element_type=F32), cos, sin_signed)
    k = _rope(jnp.dot(s, wk_ref[...], preferred_element_type=F32), cos, sin_signed)
    for lb in range(len(q)):
        lanes = slice(lb * LANES, (lb + 1) * LANES)
        q_ref[:, lanes] = (q[lb] * SCORE_SCALE).astype(q_ref.dtype)
        k_ref[:, lanes] = k[lb]
    v_ref[...] = jnp.dot(s, wv_ref[...], preferred_element_type=F32)


def _qkv(h, g_u, g_s, wq, wk, wv, cos, sin_signed, *, tm, tn, q_dtype):
    m, d = h.shape
    tm, tn = _tile(m, tm), _tile(d, tn)
    row = pl.BlockSpec((1, d), lambda i, j: (0, 0))
    wcol = pl.BlockSpec((d, tn), lambda i, j: (0, j))
    tab = pl.BlockSpec((tm, LANES), lambda i, j: (i, 0))
    out = pl.BlockSpec((tm, tn), lambda i, j: (i, j))
    return pl.pallas_call(
        _qkv_kernel,
        out_shape=(jax.ShapeDtypeStruct((m, d), q_dtype), jax.ShapeDtypeStruct((m, d), F32),
                   jax.ShapeDtypeStruct((m, d), F32)),
        grid=(m // tm, d // tn),
        in_specs=[pl.BlockSpec((tm, d), lambda i, j: (i, 0)), row, row, wcol, wcol, wcol, tab, tab],
        out_specs=(out, out, out),
        scratch_shapes=[pltpu.VMEM((tm, d), BF16), pltpu.VMEM((tm, d), BF16)],
        compiler_params=_params("arbitrary", "arbitrary"),
        name="qkv_rope",
    )(h, g_u, g_s, wq, wk, wv, cos, sin_signed)


def _rope_tables(pos):
    inv = 1.0 / (ROPE_THETA ** (jnp.arange(ROPE_HALF, dtype=F32) / ROPE_HALF))
    ang = pos.astype(F32)[:, None] * inv[None, :]
    cos, sin = jnp.cos(ang), jnp.sin(ang)
    reps = LANES // HEAD_DIM
    cos_l = jnp.tile(jnp.concatenate([cos, cos], axis=-1), (1, reps))
    sin_l = jnp.tile(jnp.concatenate([-sin, sin], axis=-1), (1, reps))
    return cos_l, sin_l


def _lambda(lam_ref, lam_init):
    lam = lam_ref[...]
    d1 = jnp.sum(lam[0:1] * lam[1:2], axis=-1, keepdims=True)
    d2 = jnp.sum(lam[2:3] * lam[3:4], axis=-1, keepdims=True)
    return jnp.exp(d1) - jnp.exp(d2) + lam_init


ROW_BLOCKS = 2


def _attn_prompt_kernel(q_ref, k_ref, v_ref, lam_ref, gsub_ref, o_ref, kb_ref, vb_ref, *, lam_init):
    tq = q_ref.shape[0]
    i = pl.program_id(2)
    nt = (((1,), (1,)), ((), ()))

    @pl.when(i == 0)
    def _():
        kb_ref[...] = k_ref[...].astype(BF16)
        vb_ref[:, 0:HEAD_W] = v_ref[...].astype(BF16)
        vb_ref[:, HEAD_W:] = jnp.ones((vb_ref.shape[0], HEAD_W), BF16)

    def attend(tile):
        past = tile * tq
        tr = tq // ROW_BLOCKS
        row = lax.broadcasted_iota(jnp.int32, (tr, tr), 0)
        col = lax.broadcasted_iota(jnp.int32, (tr, tr), 1)
        lam = _lambda(lam_ref, lam_init)
        for rb in range(ROW_BLOCKS):
            before = past + rb * tr
            q = q_ref[rb * tr:(rb + 1) * tr, :]
            lane = lax.broadcasted_iota(jnp.int32, q.shape, 1)
            zero = jnp.zeros_like(q)
            out = []
            for c in range(2):
                qc = jnp.where((lane // HEAD_DIM) == c, q, zero)
                s_diag = lax.dot_general(qc, kb_ref[before:before + tr, :], nt, preferred_element_type=F32)
                s_diag = jnp.where(col <= row, s_diag, -jnp.inf)
                m = jnp.max(s_diag, axis=-1, keepdims=True)
                if before:
                    s_past = lax.dot_general(qc, kb_ref[0:before, :], nt, preferred_element_type=F32)
                    m = jnp.maximum(m, jnp.max(s_past, axis=-1, keepdims=True))
                acc = jnp.dot(jnp.exp2(s_diag - m).astype(BF16), vb_ref[before:before + tr, :],
                              preferred_element_type=F32)
                if before:
                    acc = acc + jnp.dot(jnp.exp2(s_past - m).astype(BF16), vb_ref[0:before, :],
                                        preferred_element_type=F32)
                out.append(acc[:, 0:HEAD_W] / acc[:, HEAD_W:])
            o = out[0] - lam * out[1]
            o_ref[rb * tr:(rb + 1) * tr, :] = (
                _rms_scale(o) * gsub_ref[...] * (1.0 - lam_init)).astype(o_ref.dtype)

    for tile in range(kb_ref.shape[0] // tq):
        pl.when(i == tile)(functools.partial(attend, tile))


def _attn_prompt(q, k, v, lam_vecs, g_sub, *, batch, seq, tq, lam_init):
    m, d = q.shape
    heads = d // HEAD_W
    tq = _tile(seq, tq)
    nq = seq // tq
    return pl.pallas_call(
        functools.partial(_attn_prompt_kernel, lam_init=lam_init),
        out_shape=jax.ShapeDtypeStruct((m, d), BF16),
        grid=(batch, heads, nq),
        in_specs=[
            pl.BlockSpec((tq, HEAD_W), lambda b, h, i: (b * nq + i, h)),
            pl.BlockSpec((seq, HEAD_W), lambda b, h, i: (b, h)),
            pl.BlockSpec((seq, HEAD_W), lambda b, h, i: (b, h)),
            pl.BlockSpec(lam_vecs.shape, lambda b, h, i: (0, 0)),
            pl.BlockSpec((1, HEAD_W), lambda b, h, i: (0, 0)),
        ],
        out_specs=pl.BlockSpec((tq, HEAD_W), lambda b, h, i: (b * nq + i, h)),
        scratch_shapes=[pltpu.VMEM((seq, HEAD_W), BF16), pltpu.VMEM((seq, 2 * HEAD_W), BF16)],
        compiler_params=_params("arbitrary", "arbitrary", "arbitrary"),
        name="attn_prompt",
    )(q, k, v, lam_vecs, g_sub)


PAGES_PER_STEP = 4
PAGE_SLOTS = 3


def _attn_sample_kernel(pt_ref, q_ref, kn_ref, vn_ref, ck_hbm, cv_hbm, lam_ref, gsub_ref, o_ref,
                        kbuf_ref, vbuf_ref, sem_ref, qrows_ref, bias_ref, m_ref, l_ref, acc_ref,
                        *, lam_init):
    t_new, heads = q_ref.shape[1], q_ref.shape[2]
    rows = 2 * t_new * heads
    n_group, page = kbuf_ref.shape[1], kbuf_ref.shape[2]
    step = pl.program_id(1)
    nt = (((1,), (1,)), ((), ()))

    n_steps = pl.num_programs(0) * pl.num_programs(1)
    cur = pl.program_id(0) * pl.num_programs(1) + step

    def page_copies(group):
        slot = group % PAGE_SLOTS
        copies = []
        for g in range(n_group):
            pg = pt_ref[group * n_group + g]
            copies.append(pltpu.make_async_copy(ck_hbm.at[pg], kbuf_ref.at[slot, g], sem_ref.at[slot, 0, g]))
            copies.append(pltpu.make_async_copy(cv_hbm.at[pg], vbuf_ref.at[slot, g], sem_ref.at[slot, 1, g]))
        return copies

    def start(group):
        for c in page_copies(group):
            c.start()

    @pl.when(cur == 0)
    def _():
        for group in range(PAGE_SLOTS - 1):
            start(group)

    @pl.when(cur + (PAGE_SLOTS - 1) < n_steps)
    def _():
        start(cur + (PAGE_SLOTS - 1))

    for c in page_copies(cur):
        c.wait()
    slot = cur % PAGE_SLOTS

    @pl.when(step == 0)
    def _():
        lane = lax.broadcasted_iota(jnp.int32, (heads, HEAD_W), 1)
        for c in range(2):
            own = (lane // HEAD_DIM) == c
            for t in range(t_new):
                r0 = (c * t_new + t) * heads
                qrows_ref[r0:r0 + heads, :] = jnp.where(own, q_ref[0, t], 0.0).astype(BF16)
        row_head = lax.broadcasted_iota(jnp.int32, bias_ref.shape, 0) % heads
        key_head = lax.broadcasted_iota(jnp.int32, bias_ref.shape, 1) % heads
        bias_ref[...] = jnp.where(row_head == key_head, 0.0, -jnp.inf)
        m_ref[...] = jnp.full(m_ref.shape, -jnp.inf, F32)
        l_ref[...] = jnp.zeros(l_ref.shape, F32)
        acc_ref[...] = jnp.zeros(acc_ref.shape, F32)

    def partial_softmax(s, v):
        m = jnp.max(s, axis=-1, keepdims=True)
        p = jnp.exp2(s - m)
        return m, jnp.sum(p, axis=-1, keepdims=True), jnp.dot(p.astype(BF16), v, preferred_element_type=F32)

    def merge(parts):
        m_prev = m_ref[...]
        m_new = m_prev
        for m, _, _ in parts:
            m_new = jnp.maximum(m_new, m)
        alpha = jnp.exp2(m_prev - m_new)
        l_new = alpha * l_ref[...]
        acc = alpha * acc_ref[...]
        for m, l, a in parts:
            w = jnp.exp2(m - m_new)
            l_new = l_new + w * l
            acc = acc + w * a
        m_ref[...], l_ref[...], acc_ref[...] = m_new, l_new, acc

    q_rows = qrows_ref[...]
    bias = bias_ref[...]
    parts = []
    for g in range(n_group):
        keys = kbuf_ref[slot, g].reshape(page * heads, HEAD_W).astype(BF16)
        s = lax.dot_general(q_rows, keys, nt, preferred_element_type=F32) + bias
        parts.append(partial_softmax(s, vbuf_ref[slot, g].reshape(page * heads, HEAD_W).astype(BF16)))
    merge(parts)

    @pl.when(step == pl.num_programs(1) - 1)
    def _():
        keys = kn_ref[0].reshape(t_new * heads, HEAD_W).astype(BF16)
        s = lax.dot_general(q_rows, keys, nt, preferred_element_type=F32)
        row = lax.broadcasted_iota(jnp.int32, s.shape, 0)
        col = lax.broadcasted_iota(jnp.int32, s.shape, 1)
        ok = (row % heads == col % heads) & (col // heads <= (row // heads) % t_new)
        merge([partial_softmax(jnp.where(ok, s, -jnp.inf),
                               vn_ref[0].reshape(t_new * heads, HEAD_W).astype(BF16))])

        half = rows // 2
        lam = _lambda(lam_ref, lam_init)
        o = acc_ref[...] / l_ref[...]
        w = o[0:half] - lam * o[half:rows]
        w = _rms_scale(w) * gsub_ref[...] * (1.0 - lam_init)
        o_ref[0] = w.reshape(t_new, heads, HEAD_W)


def _attn_sample(q, k_new, v_new, cache_k, cache_v, page_table, lam_vecs, g_sub, *, lam_init):
    b, t_new, heads, _ = q.shape
    n_pages = page_table.shape[1]
    page = cache_k.shape[1]
    rows = 2 * t_new * heads
    n_group = math.gcd(n_pages, PAGES_PER_STEP)
    steps = n_pages // n_group
    assert b * steps >= PAGE_SLOTS
    tok = pl.BlockSpec((1, t_new, heads, HEAD_W), lambda bi, s, pt: (bi, 0, 0, 0))
    hbm = pl.BlockSpec(memory_space=pl.ANY)
    grid_spec = pltpu.PrefetchScalarGridSpec(
        num_scalar_prefetch=1,
        grid=(b, steps),
        in_specs=[tok, tok, tok, hbm, hbm,
                  pl.BlockSpec(lam_vecs.shape, lambda bi, s, pt: (0, 0)),
                  pl.BlockSpec((1, HEAD_W), lambda bi, s, pt: (0, 0))],
        out_specs=tok,
        scratch_shapes=[pltpu.VMEM((PAGE_SLOTS, n_group, page, heads, HEAD_W), cache_k.dtype),
                        pltpu.VMEM((PAGE_SLOTS, n_group, page, heads, HEAD_W), cache_v.dtype),
                        pltpu.SemaphoreType.DMA((PAGE_SLOTS, 2, n_group)),
                        pltpu.VMEM((rows, HEAD_W), BF16), pltpu.VMEM((rows, page * heads), F32),
                        pltpu.VMEM((rows, 1), F32), pltpu.VMEM((rows, 1), F32),
                        pltpu.VMEM((rows, HEAD_W), F32)],
    )
    return pl.pallas_call(
        functools.partial(_attn_sample_kernel, lam_init=lam_init),
        out_shape=jax.ShapeDtypeStruct((b, t_new, heads, HEAD_W), F32),
        grid_spec=grid_spec,
        compiler_params=_params("arbitrary", "arbitrary"),
        name="attn_sample",
    )(page_table.reshape(-1), q, k_new, v_new, cache_k, cache_v, lam_vecs, g_sub)


def _row(v):
    return v.reshape(1, -1).astype(F32)


def _post_mixer(h, p, i, W, tm):
    h = _ffn(h, _row(W['g_ffn_pre'][i]), W['w_ffn_gate'][i], W['w_ffn_up'][i], W['w_ffn_down'][i],
             _row(W['g_ffn_post'][i]), tm=tm, tf=512)
    return _ple(h, p[i], W['w_ple_gate'][i], W['w_ple_proj'][i], _row(W['g_ple_post'][i]), tm=tm)


def _conv_layer(x, conv_fn, W, tm):
    glu = _pw1_glu(x, _row(W['g_mix_pre'][0]), W['w_pw1'][0], _row(W['b_pw1'][0]), tm=tm, tn=512)
    act, extra = conv_fn(glu)
    h = _mm_norm_res(act, W['w_pw2'][0], _row(W['b_pw2'][0]), _row(W['g_mix_post'][0]), x, tm=tm)
    return h, extra


def kernel(x_prompt, x_sample, p_prompt, p_sample, state_conv, cache_k, cache_v, page_table, g_mix_pre, g_mix_post, g_ffn_pre, g_ffn_post, g_ple_post, w_pw1, b_pw1, w_dw, b_dw, g_conv_ln, b_conv_ln, w_pw2, b_pw2, g_kv_norm, w_k, w_v, w_q, lam_q1, lam_k1, lam_q2, lam_k2, g_subln, w_o, w_ffn_gate, w_ffn_up, w_ffn_down, w_ple_proj, w_ple_gate):
    B, S, D = x_prompt.shape
    DB, T, _ = x_sample.shape
    heads = D // HEAD_W
    page = cache_k.shape[1]
    assert state_conv.shape[0] == 1 and w_q.shape[0] == 1, "one conv layer then one attention layer"

    W = {
        'g_mix_pre': g_mix_pre, 'g_mix_post': g_mix_post, 'g_ffn_pre': g_ffn_pre,
        'g_ffn_post': g_ffn_post, 'g_ple_post': g_ple_post, 'b_pw1': b_pw1, 'b_pw2': b_pw2,
    }
    for name, w in (('w_pw1', w_pw1), ('w_pw2', w_pw2), ('w_ffn_gate', w_ffn_gate), ('w_ffn_up', w_ffn_up),
                    ('w_ffn_down', w_ffn_down), ('w_ple_proj', w_ple_proj), ('w_ple_gate', w_ple_gate)):
        W[name] = [w[i].astype(BF16) for i in range(w.shape[0])]
    wq, wk, wv, wo = w_q[0].astype(BF16), w_k.astype(BF16), w_v.astype(BF16), w_o[0].astype(BF16)
    w_dw0, b_dw0 = w_dw[0], _row(b_dw[0])
    g_ln0, b_ln0 = _row(g_conv_ln[0]), _row(b_conv_ln[0])
    lam_vecs = jnp.concatenate([lam_q1, lam_k1, lam_q2, lam_k2], axis=0).astype(F32)
    lam_init = 0.8 - 0.6 * math.exp(-0.3 * 1)
    g_sub = _row(g_subln[0])

    def attention_layer(h, pos, attn_fn, tm, q_dtype):
        cos, sin_signed = _rope_tables(pos)
        q, k, v = _qkv(h, _row(g_mix_pre[1]), _row(g_kv_norm), wq, wk, wv, cos, sin_signed,
                       tm=tm, tn=512, q_dtype=q_dtype)
        o = attn_fn(q, k, v)
        h = _mm_norm_res(o, wo, None, _row(g_mix_post[1]), h, tm=tm)
        return h, k, v

    tm_p = 512
    xp = x_prompt.reshape(B * S, D)
    pp = p_prompt.reshape(p_prompt.shape[0], B * S, -1)

    def conv_p(glu):
        act = _conv_prompt(glu, w_dw0, b_dw0, g_ln0, b_ln0, seq=S, tm=tm_p)
        return act, glu.reshape(B, S, D)[:, S - CONV_HIST:, :][None]

    h, conv_state_p = _conv_layer(xp, conv_p, W, tm_p)
    h = _post_mixer(h, pp, 0, W, tm_p)
    h, k_p, v_p = attention_layer(
        h, jnp.tile(jnp.arange(S), B),
        lambda q, k, v: _attn_prompt(q, k, v, lam_vecs, g_sub, batch=B, seq=S, tq=512, lam_init=lam_init),
        tm_p, BF16)
    y_p = _post_mixer(h, pp, 1, W, tm_p).reshape(B, S, D)

    tm_s = 512
    xs = x_sample.reshape(DB * T, D)
    ps = p_sample.reshape(p_sample.shape[0], DB * T, -1)
    past_len = page_table.shape[1] * page

    def conv_s(glu):
        return _conv_sample(state_conv[0], glu, w_dw0, b_dw0, g_ln0, b_ln0, nb=8)

    def attn_s(q, k, v):
        per_head = (DB, T, heads, HEAD_W)
        o = _attn_sample(q.reshape(per_head), k.reshape(per_head), v.reshape(per_head),
                         cache_k, cache_v, page_table, lam_vecs, g_sub, lam_init=lam_init)
        return o.reshape(DB * T, D)

    h, conv_state_s = _conv_layer(xs, conv_s, W, tm_s)
    h = _post_mixer(h, ps, 0, W, tm_s)
    h, k_s, v_s = attention_layer(h, jnp.tile(past_len + jnp.arange(T), DB), attn_s, tm_s, F32)
    y_s = _post_mixer(h, ps, 1, W, tm_s).reshape(DB, T, D)

    return (y_p, y_s, conv_state_p,
            k_p.reshape(B, S, heads, HEAD_W), v_p.reshape(B, S, heads, HEAD_W),
            conv_state_s[None],
            k_s.reshape(DB, T, heads, HEAD_W), v_s.reshape(DB, T, heads, HEAD_W))
```

```python
import functools
import math

import jax
import jax.numpy as jnp
from jax import lax
from jax.experimental import pallas as pl
from jax.experimental.pallas import tpu as pltpu

F32 = jnp.float32
BF16 = jnp.bfloat16

RMS_EPS = 1e-6
LN_EPS = 1e-5
ROPE_THETA = 10000.0
HEAD_DIM = 64
HEAD_W = 2 * HEAD_DIM
ROPE_HALF = HEAD_DIM // 2
SCORE_SCALE = HEAD_DIM ** -0.5 * math.log2(math.e)
CONV_W = 31
CONV_HIST = CONV_W - 1
CONV_HALO = 32
SUBLANES = 8
LANES = 128
V7X_VMEM_LIMIT = 56 * 1024 * 1024
ROW_TILE_STREAMED = 1024


def _params(*sem):
    return pltpu.CompilerParams(dimension_semantics=sem, vmem_limit_bytes=V7X_VMEM_LIMIT)


def _sigmoid(x):
    return 1.0 / (1.0 + jnp.exp(-x))


def _rms_scale(x):
    return x * lax.rsqrt(jnp.mean(x * x, axis=-1, keepdims=True) + RMS_EPS)


def _tile(m, want):
    t = min(m, want)
    assert m % t == 0, (m, t)
    return t


def _pw1_glu_kernel(x_ref, g_ref, wa_ref, wg_ref, ba_ref, bg_ref, o_ref, u_ref):
    @pl.when(pl.program_id(1) == 0)
    def _():
        u_ref[...] = (_rms_scale(x_ref[...]) * g_ref[...]).astype(BF16)

    u = u_ref[...]
    a = jnp.dot(u, wa_ref[...], preferred_element_type=F32) + ba_ref[...]
    gt = jnp.dot(u, wg_ref[...], preferred_element_type=F32) + bg_ref[...]
    o_ref[...] = a * _sigmoid(gt)


def _pw1_glu(x, g, w, b, *, tm, tn):
    m, d = x.shape
    tm, tn = _tile(m, tm), _tile(d, tn)
    nj = d // tn
    return pl.pallas_call(
        _pw1_glu_kernel,
        out_shape=jax.ShapeDtypeStruct((m, d), F32),
        grid=(m // tm, nj),
        in_specs=[
            pl.BlockSpec((tm, d), lambda i, j: (i, 0)),
            pl.BlockSpec((1, d), lambda i, j: (0, 0)),
            pl.BlockSpec((d, tn), lambda i, j: (0, j)),
            pl.BlockSpec((d, tn), lambda i, j: (0, j + nj)),
            pl.BlockSpec((1, tn), lambda i, j: (0, j)),
            pl.BlockSpec((1, tn), lambda i, j: (0, j + nj)),
        ],
        out_specs=pl.BlockSpec((tm, tn), lambda i, j: (i, j)),
        scratch_shapes=[pltpu.VMEM((tm, d), BF16)],
        compiler_params=_params("arbitrary", "arbitrary"),
        name="pw1_glu",
    )(x, g, w, w, b, b)


def _ln_silu(c, g, b):
    mu = jnp.mean(c, axis=-1, keepdims=True)
    xc = c - mu
    y = xc * lax.rsqrt(jnp.mean(xc * xc, axis=-1, keepdims=True) + LN_EPS) * g + b
    return y * _sigmoid(y)


CONV_ROWS_PER_ITER = 8 * SUBLANES


def _conv_prompt_kernel(cur_ref, prev_ref, w_ref, bdw_ref, gln_ref, bln_ref, o_ref, buf_ref, c_ref,
                        *, tiles_per_seq):
    tm, d = cur_ref.shape
    first = (pl.program_id(0) % tiles_per_seq) == 0
    lead = CONV_HALO - CONV_HIST
    groups = CONV_ROWS_PER_ITER // SUBLANES

    for lb in range(d // LANES):
        lanes = slice(lb * LANES, (lb + 1) * LANES)
        buf_ref[lb, 0:CONV_HALO, :] = jnp.where(first, 0.0, prev_ref[:, lanes])
        buf_ref[lb, CONV_HALO:, :] = cur_ref[:, lanes]
        taps = [jnp.broadcast_to(w_ref[j:j + 1, lanes], (SUBLANES, LANES)) for j in range(CONV_W)]
        bias = jnp.broadcast_to(bdw_ref[:, lanes], (SUBLANES, LANES))

        def body(r, carry, lb=lb, taps=taps, bias=bias):
            r0 = pl.multiple_of(r * CONV_ROWS_PER_ITER, CONV_ROWS_PER_ITER)
            for q in range(groups):
                base = r0 + (q // 2) * 2 * SUBLANES + (q % 2)
                acc = bias
                for j in range(CONV_W):
                    acc = acc + taps[j] * buf_ref[lb, pl.ds(base + lead + j, SUBLANES, stride=2), :]
                c_ref[lb, pl.ds(base, SUBLANES, stride=2), :] = acc
            return carry

        lax.fori_loop(0, tm // CONV_ROWS_PER_ITER, body, 0)

    c = c_ref[...]
    mu = jnp.sum(jnp.sum(c, axis=0), axis=-1, keepdims=True) / d
    xc = c - mu[None]
    var = jnp.sum(jnp.sum(xc * xc, axis=0), axis=-1, keepdims=True) / d
    y = xc * lax.rsqrt(var + LN_EPS)[None] * gln_ref[...] + bln_ref[...]
    act = y * _sigmoid(y)
    for lb in range(d // LANES):
        o_ref[:, lb * LANES:(lb + 1) * LANES] = act[lb].astype(BF16)


def _conv_prompt(glu, w_dw, b_dw, g_ln, b_ln, *, seq, tm):
    m, d = glu.shape
    tm = _tile(seq, tm)
    halo_blocks = tm // CONV_HALO
    nlb = d // LANES
    chan = pl.BlockSpec((nlb, 1, LANES), lambda i: (0, 0, 0))
    return pl.pallas_call(
        functools.partial(_conv_prompt_kernel, tiles_per_seq=seq // tm),
        out_shape=jax.ShapeDtypeStruct((m, d), BF16),
        grid=(m // tm,),
        in_specs=[
            pl.BlockSpec((tm, d), lambda i: (i, 0)),
            pl.BlockSpec((CONV_HALO, d), lambda i: (jnp.maximum(i * halo_blocks - 1, 0), 0)),
            pl.BlockSpec((CONV_W, d), lambda i: (0, 0)),
            pl.BlockSpec((1, d), lambda i: (0, 0)),
            chan,
            chan,
        ],
        out_specs=pl.BlockSpec((tm, d), lambda i: (i, 0)),
        scratch_shapes=[pltpu.VMEM((nlb, CONV_HALO + tm, LANES), F32), pltpu.VMEM((nlb, tm, LANES), F32)],
        compiler_params=_params("arbitrary"),
        name="conv_prompt",
    )(glu, glu, w_dw, b_dw, g_ln.reshape(nlb, 1, LANES), b_ln.reshape(nlb, 1, LANES))


def _conv_sample_kernel(st_ref, glu_ref, w_ref, bdw_ref, gln_ref, bln_ref, o_ref, ns_ref, pad_ref, c_ref,
                        *, t_new):
    nb = st_ref.shape[0]
    w = w_ref[...]
    for bi in range(nb):
        new = glu_ref[bi * t_new:(bi + 1) * t_new, :]
        pad_ref[0:CONV_HIST, :] = st_ref[bi]
        pad_ref[CONV_HIST:CONV_HIST + t_new, :] = new
        for t in range(t_new):
            c_ref[bi * t_new + t:bi * t_new + t + 1, :] = (
                jnp.sum(pad_ref[t:t + CONV_W, :] * w, axis=0, keepdims=True) + bdw_ref[...])
        ns_ref[bi] = pad_ref[t_new:t_new + CONV_HIST, :]
    o_ref[...] = _ln_silu(c_ref[...], gln_ref[...], bln_ref[...]).astype(BF16)


def _conv_sample(state, glu, w_dw, b_dw, g_ln, b_ln, *, nb):
    b, hist, d = state.shape
    t_new = glu.shape[0] // b
    nb = _tile(b, nb)
    rows = nb * t_new
    return pl.pallas_call(
        functools.partial(_conv_sample_kernel, t_new=t_new),
        out_shape=(jax.ShapeDtypeStruct((b * t_new, d), BF16), jax.ShapeDtypeStruct((b, hist, d), F32)),
        grid=(b // nb,),
        in_specs=[
            pl.BlockSpec((nb, hist, d), lambda i: (i, 0, 0)),
            pl.BlockSpec((rows, d), lambda i: (i, 0)),
            pl.BlockSpec((CONV_W, d), lambda i: (0, 0)),
            pl.BlockSpec((1, d), lambda i: (0, 0)),
            pl.BlockSpec((1, d), lambda i: (0, 0)),
            pl.BlockSpec((1, d), lambda i: (0, 0)),
        ],
        out_specs=(pl.BlockSpec((rows, d), lambda i: (i, 0)),
                   pl.BlockSpec((nb, hist, d), lambda i: (i, 0, 0))),
        scratch_shapes=[pltpu.VMEM((CONV_HIST + t_new + SUBLANES, d), F32), pltpu.VMEM((rows, d), F32)],
        compiler_params=_params("arbitrary"),
        name="conv_sample",
    )(state, glu, w_dw, b_dw, g_ln, b_ln)


def _mm_norm_res_kernel(*refs, has_bias):
    if has_bias:
        a_ref, w_ref, b_ref, g_ref, r_ref, o_ref = refs
    else:
        a_ref, w_ref, g_ref, r_ref, o_ref = refs
    m = jnp.dot(a_ref[...].astype(BF16), w_ref[...], preferred_element_type=F32)
    if has_bias:
        m = m + b_ref[...]
    o_ref[...] = r_ref[...] + _rms_scale(m) * g_ref[...]


def _mm_norm_res(a, w, bias, g, res, *, tm):
    m, k = a.shape
    d = w.shape[1]
    tm = _tile(m, tm)
    row = pl.BlockSpec((1, d), lambda i: (0, 0))
    has_bias = bias is not None
    in_specs = [pl.BlockSpec((tm, k), lambda i: (i, 0)), pl.BlockSpec((k, d), lambda i: (0, 0))]
    args = [a, w]
    if has_bias:
        in_specs.append(row)
        args.append(bias)
    in_specs += [row, pl.BlockSpec((tm, d), lambda i: (i, 0))]
    args += [g, res]
    return pl.pallas_call(
        functools.partial(_mm_norm_res_kernel, has_bias=has_bias),
        out_shape=jax.ShapeDtypeStruct((m, d), F32),
        grid=(m // tm,),
        in_specs=in_specs,
        out_specs=pl.BlockSpec((tm, d), lambda i: (i, 0)),
        compiler_params=_params("arbitrary"),
        name="mm_norm_res",
    )(*args)


def _ffn_kernel(h_ref, gpre_ref, wg_ref, wu_ref, wd_ref, gpost_ref, o_ref, z_ref, acc_ref):
    c = pl.program_id(1)

    @pl.when(c == 0)
    def _():
        z_ref[...] = (_rms_scale(h_ref[...]) * gpre_ref[...]).astype(BF16)
        acc_ref[...] = jnp.zeros(acc_ref.shape, F32)

    z = z_ref[...]
    gate = jnp.dot(z, wg_ref[...], preferred_element_type=F32)
    up = jnp.dot(z, wu_ref[...], preferred_element_type=F32)
    act = (gate * _sigmoid(gate) * up).astype(BF16)
    acc_ref[...] += jnp.dot(act, wd_ref[...], preferred_element_type=F32)

    @pl.when(c == pl.num_programs(1) - 1)
    def _():
        o_ref[...] = h_ref[...] + _rms_scale(acc_ref[...]) * gpost_ref[...]


def _ffn(h, g_pre, wg, wu, wd, g_post, *, layer, tm, tf):
    m, d = h.shape
    f = wg.shape[2]
    tm, tf = _tile(m, tm), _tile(f, tf)
    row = pl.BlockSpec((1, d), lambda i, c: (0, 0))
    return pl.pallas_call(
        _ffn_kernel,
        out_shape=jax.ShapeDtypeStruct((m, d), F32),
        grid=(m // tm, f // tf),
        in_specs=[
            pl.BlockSpec((tm, d), lambda i, c: (i, 0)),
            row,
            pl.BlockSpec((None, d, tf), lambda i, c: (layer, 0, c)),
            pl.BlockSpec((None, d, tf), lambda i, c: (layer, 0, c)),
            pl.BlockSpec((None, tf, d), lambda i, c: (layer, c, 0)),
            row,
        ],
        out_specs=pl.BlockSpec((tm, d), lambda i, c: (i, 0)),
        scratch_shapes=[pltpu.VMEM((tm, d), BF16), pltpu.VMEM((tm, d), F32)],
        compiler_params=_params("arbitrary", "arbitrary"),
        name="ffn",
    )(h, g_pre, wg, wu, wd, g_post)


def _ple_kernel(h_ref, p_ref, wgate_ref, wproj_ref, g_ref, o_ref):
    h = h_ref[...]
    gate = _sigmoid(jnp.dot(h.astype(BF16), wgate_ref[...], preferred_element_type=F32))
    e = jnp.dot(p_ref[...].astype(BF16), wproj_ref[...], preferred_element_type=F32)
    o_ref[...] = h + _rms_scale(gate * e) * g_ref[...]


def _ple(h, p, w_gate, w_proj, g, *, tm):
    m, d = h.shape
    pd = p.shape[1]
    tm = _tile(m, tm)
    return pl.pallas_call(
        _ple_kernel,
        out_shape=jax.ShapeDtypeStruct((m, d), F32),
        grid=(m // tm,),
        in_specs=[
            pl.BlockSpec((tm, d), lambda i: (i, 0)),
            pl.BlockSpec((tm, pd), lambda i: (i, 0)),
            pl.BlockSpec((d, d), lambda i: (0, 0)),
            pl.BlockSpec((pd, d), lambda i: (0, 0)),
            pl.BlockSpec((1, d), lambda i: (0, 0)),
        ],
        out_specs=pl.BlockSpec((tm, d), lambda i: (i, 0)),
        compiler_params=_params("arbitrary"),
        name="ple",
    )(h, p, w_gate, w_proj, g)


def _rope(y, cos, sin_signed):
    lane = lax.broadcasted_iota(jnp.int32, (y.shape[0], LANES), 1)
    low_half = (lane % HEAD_DIM) < ROPE_HALF
    out = []
    for lb in range(y.shape[1] // LANES):
        yb = y[:, lb * LANES:(lb + 1) * LANES]
        partner = jnp.where(low_half, pltpu.roll(yb, LANES - ROPE_HALF, 1), pltpu.roll(yb, ROPE_HALF, 1))
        out.append(yb * cos + partner * sin_signed)
    return out


def _qkv_kernel(h_ref, gu_ref, gs_ref, wq_ref, wk_ref, wv_ref, cos_ref, sin_ref,
                q_ref, k_ref, v_ref, u_ref, s_ref):
    @pl.when(pl.program_id(1) == 0)
    def _():
        r = _rms_scale(h_ref[...])
        u_ref[...] = (r * gu_ref[...]).astype(BF16)
        s_ref[...] = (r * gs_ref[...]).astype(BF16)

    cos, sin_signed = cos_ref[...], sin_ref[...]
    s = s_ref[...]
    q = _rope(jnp.dot(u_ref[...], wq_ref[...], preferred_element_type=F32), cos, sin_signed)
    k = _rope(jnp.dot(s, wk_ref[...], preferred_element_type=F32), cos, sin_signed)
    for lb in range(len(q)):
        lanes = slice(lb * LANES, (lb + 1) * LANES)
        q_ref[:, lanes] = (q[lb] * SCORE_SCALE).astype(q_ref.dtype)
        k_ref[:, lanes] = k[lb]
    v_ref[...] = jnp.dot(s, wv_ref[...], preferred_element_type=F32)


def _qkv(h, g_u, g_s, wq, wk, wv, cos, sin_signed, *, tm, tn, q_dtype):
    m, d = h.shape
    tm, tn = _tile(m, tm), _tile(d, tn)
    row = pl.BlockSpec((1, d), lambda i, j: (0, 0))
    wcol = pl.BlockSpec((d, tn), lambda i, j: (0, j))
    tab = pl.BlockSpec((tm, LANES), lambda i, j: (i, 0))
    out = pl.BlockSpec((tm, tn), lambda i, j: (i, j))
    return pl.pallas_call(
        _qkv_kernel,
        out_shape=(jax.ShapeDtypeStruct((m, d), q_dtype), jax.ShapeDtypeStruct((m, d), F32),
                   jax.ShapeDtypeStruct((m, d), F32)),
        grid=(m // tm, d // tn),
        in_specs=[pl.BlockSpec((tm, d), lambda i, j: (i, 0)), row, row, wcol, wcol, wcol, tab, tab],
        out_specs=(out, out, out),
        scratch_shapes=[pltpu.VMEM((tm, d), BF16), pltpu.VMEM((tm, d), BF16)],
        compiler_params=_params("arbitrary", "arbitrary"),
        name="qkv_rope",
    )(h, g_u, g_s, wq, wk, wv, cos, sin_signed)


def _rope_tables(pos):
    inv = 1.0 / (ROPE_THETA ** (jnp.arange(ROPE_HALF, dtype=F32) / ROPE_HALF))
    ang = pos.astype(F32)[:, None] * inv[None, :]
    cos, sin = jnp.cos(ang), jnp.sin(ang)
    reps = LANES // HEAD_DIM
    cos_l = jnp.tile(jnp.concatenate([cos, cos], axis=-1), (1, reps))
    sin_l = jnp.tile(jnp.concatenate([-sin, sin], axis=-1), (1, reps))
    return cos_l, sin_l


def _lambda(lam_ref, lam_init):
    lam = lam_ref[...]
    d1 = jnp.sum(lam[0:1] * lam[1:2], axis=-1, keepdims=True)
    d2 = jnp.sum(lam[2:3] * lam[3:4], axis=-1, keepdims=True)
    return jnp.exp(d1) - jnp.exp(d2) + lam_init


ROW_BLOCKS = 2


def _attn_prompt_kernel(q_ref, k_ref, v_ref, lam_ref, gsub_ref, o_ref, kb_ref, vb_ref, *, lam_init):
    tq = q_ref.shape[0]
    i = pl.program_id(2)
    nt = (((1,), (1,)), ((), ()))

    @pl.when(i == 0)
    def _():
        kb_ref[...] = k_ref[...].astype(BF16)
        vb_ref[:, 0:HEAD_W] = v_ref[...].astype(BF16)
        vb_ref[:, HEAD_W:] = jnp.ones((vb_ref.shape[0], HEAD_W), BF16)

    def attend(tile):
        past = tile * tq
        tr = tq // ROW_BLOCKS
        row = lax.broadcasted_iota(jnp.int32, (tr, tr), 0)
        col = lax.broadcasted_iota(jnp.int32, (tr, tr), 1)
        lam = _lambda(lam_ref, lam_init)
        for rb in range(ROW_BLOCKS):
            before = past + rb * tr
            q = q_ref[rb * tr:(rb + 1) * tr, :]
            lane = lax.broadcasted_iota(jnp.int32, q.shape, 1)
            zero = jnp.zeros_like(q)
            out = []
            for c in range(2):
                qc = jnp.where((lane // HEAD_DIM) == c, q, zero)
                s_diag = lax.dot_general(qc, kb_ref[before:before + tr, :], nt, preferred_element_type=F32)
                s_diag = jnp.where(col <= row, s_diag, -jnp.inf)
                m = jnp.max(s_diag, axis=-1, keepdims=True)
                if before:
                    s_past = lax.dot_general(qc, kb_ref[0:before, :], nt, preferred_element_type=F32)
                    m = jnp.maximum(m, jnp.max(s_past, axis=-1, keepdims=True))
                acc = jnp.dot(jnp.exp2(s_diag - m).astype(BF16), vb_ref[before:before + tr, :],
                              preferred_element_type=F32)
                if before:
                    acc = acc + jnp.dot(jnp.exp2(s_past - m).astype(BF16), vb_ref[0:before, :],
                                        preferred_element_type=F32)
                out.append(acc[:, 0:HEAD_W] / acc[:, HEAD_W:])
            o = out[0] - lam * out[1]
            o_ref[rb * tr:(rb + 1) * tr, :] = (
                _rms_scale(o) * gsub_ref[...] * (1.0 - lam_init)).astype(o_ref.dtype)

    for tile in range(kb_ref.shape[0] // tq):
        pl.when(i == tile)(functools.partial(attend, tile))


def _attn_prompt(q, k, v, lam_vecs, g_sub, *, batch, seq, tq, lam_init):
    m, d = q.shape
    heads = d // HEAD_W
    tq = _tile(seq, tq)
    nq = seq // tq
    return pl.pallas_call(
        functools.partial(_attn_prompt_kernel, lam_init=lam_init),
        out_shape=jax.ShapeDtypeStruct((m, d), BF16),
        grid=(batch, heads, nq),
        in_specs=[
            pl.BlockSpec((tq, HEAD_W), lambda b, h, i: (b * nq + i, h)),
            pl.BlockSpec((seq, HEAD_W), lambda b, h, i: (b, h)),
            pl.BlockSpec((seq, HEAD_W), lambda b, h, i: (b, h)),
            pl.BlockSpec(lam_vecs.shape, lambda b, h, i: (0, 0)),
            pl.BlockSpec((1, HEAD_W), lambda b, h, i: (0, 0)),
        ],
        out_specs=pl.BlockSpec((tq, HEAD_W), lambda b, h, i: (b * nq + i, h)),
        scratch_shapes=[pltpu.VMEM((seq, HEAD_W), BF16), pltpu.VMEM((seq, 2 * HEAD_W), BF16)],
        compiler_params=_params("arbitrary", "arbitrary", "arbitrary"),
        name="attn_prompt",
    )(q, k, v, lam_vecs, g_sub)


PAGES_PER_STEP = 4
PAGE_SLOTS = 3


def _attn_sample_kernel(pt_ref, q_ref, kn_ref, vn_ref, ck_hbm, cv_hbm, lam_ref, gsub_ref, o_ref,
                        kbuf_ref, vbuf_ref, sem_ref, qrows_ref, bias_ref, m_ref, l_ref, acc_ref,
                        *, lam_init):
    t_new, heads = q_ref.shape[1], q_ref.shape[2]
    rows = 2 * t_new * heads
    n_group, page = kbuf_ref.shape[1], kbuf_ref.shape[2]
    step = pl.program_id(1)
    nt = (((1,), (1,)), ((), ()))

    n_steps = pl.num_programs(0) * pl.num_programs(1)
    cur = pl.program_id(0) * pl.num_programs(1) + step

    def page_copies(group):
        slot = group % PAGE_SLOTS
        copies = []
        for g in range(n_group):
            pg = pt_ref[group * n_group + g]
            copies.append(pltpu.make_async_copy(ck_hbm.at[pg], kbuf_ref.at[slot, g], sem_ref.at[slot, 0, g]))
            copies.append(pltpu.make_async_copy(cv_hbm.at[pg], vbuf_ref.at[slot, g], sem_ref.at[slot, 1, g]))
        return copies

    def start(group):
        for c in page_copies(group):
            c.start()

    @pl.when(cur == 0)
    def _():
        for group in range(PAGE_SLOTS - 1):
            start(group)

    @pl.when(cur + (PAGE_SLOTS - 1) < n_steps)
    def _():
        start(cur + (PAGE_SLOTS - 1))

    for c in page_copies(cur):
        c.wait()
    slot = cur % PAGE_SLOTS

    @pl.when(step == 0)
    def _():
        lane = lax.broadcasted_iota(jnp.int32, (heads, HEAD_W), 1)
        for c in range(2):
            own = (lane // HEAD_DIM) == c
            for t in range(t_new):
                r0 = (c * t_new + t) * heads
                qrows_ref[r0:r0 + heads, :] = jnp.where(own, q_ref[0, t], 0.0).astype(BF16)
        row_head = lax.broadcasted_iota(jnp.int32, bias_ref.shape, 0) % heads
        key_head = lax.broadcasted_iota(jnp.int32, bias_ref.shape, 1) % heads
        bias_ref[...] = jnp.where(row_head == key_head, 0.0, -jnp.inf)
        m_ref[...] = jnp.full(m_ref.shape, -jnp.inf, F32)
        l_ref[...] = jnp.zeros(l_ref.shape, F32)
        acc_ref[...] = jnp.zeros(acc_ref.shape, F32)

    def partial_softmax(s, v):
        m = jnp.max(s, axis=-1, keepdims=True)
        p = jnp.exp2(s - m)
        return m, jnp.sum(p, axis=-1, keepdims=True), jnp.dot(p.astype(BF16), v, preferred_element_type=F32)

    def merge(parts):
        m_prev = m_ref[...]
        m_new = m_prev
        for m, _, _ in parts:
            m_new = jnp.maximum(m_new, m)
        alpha = jnp.exp2(m_prev - m_new)
        l_new = alpha * l_ref[...]
        acc = alpha * acc_ref[...]
        for m, l, a in parts:
            w = jnp.exp2(m - m_new)
            l_new = l_new + w * l
            acc = acc + w * a
        m_ref[...], l_ref[...], acc_ref[...] = m_new, l_new, acc

    q_rows = qrows_ref[...]
    bias = bias_ref[...]
    scores = []
    for g in range(n_group):
        keys = kbuf_ref[slot, g].reshape(page * heads, HEAD_W).astype(BF16)
        scores.append(lax.dot_general(q_rows, keys, nt, preferred_element_type=F32) + bias)
    maxes = [jnp.max(s, axis=-1, keepdims=True) for s in scores]
    probs = [jnp.exp2(s - m) for s, m in zip(scores, maxes)]
    sums = [jnp.sum(p, axis=-1, keepdims=True) for p in probs]
    outs = [jnp.dot(p.astype(BF16), vbuf_ref[slot, g].reshape(page * heads, HEAD_W).astype(BF16),
                    preferred_element_type=F32) for g, p in enumerate(probs)]
    merge(list(zip(maxes, sums, outs)))

    @pl.when(step == pl.num_programs(1) - 1)
    def _():
        keys = kn_ref[0].reshape(t_new * heads, HEAD_W).astype(BF16)
        s = lax.dot_general(q_rows, keys, nt, preferred_element_type=F32)
        row = lax.broadcasted_iota(jnp.int32, s.shape, 0)
        col = lax.broadcasted_iota(jnp.int32, s.shape, 1)
        ok = (row % heads == col % heads) & (col // heads <= (row // heads) % t_new)
        merge([partial_softmax(jnp.where(ok, s, -jnp.inf),
                               vn_ref[0].reshape(t_new * heads, HEAD_W).astype(BF16))])

        half = rows // 2
        lam = _lambda(lam_ref, lam_init)
        o = acc_ref[...] / l_ref[...]
        w = o[0:half] - lam * o[half:rows]
        w = _rms_scale(w) * gsub_ref[...] * (1.0 - lam_init)
        o_ref[0] = w.reshape(t_new, heads, HEAD_W)


def _attn_sample(q, k_new, v_new, cache_k, cache_v, page_table, lam_vecs, g_sub, *, lam_init):
    b, t_new, heads, _ = q.shape
    n_pages = page_table.shape[1]
    page = cache_k.shape[1]
    rows = 2 * t_new * heads
    n_group = math.gcd(n_pages, PAGES_PER_STEP)
    steps = n_pages // n_group
    assert b * steps >= PAGE_SLOTS
    tok = pl.BlockSpec((1, t_new, heads, HEAD_W), lambda bi, s, pt: (bi, 0, 0, 0))
    hbm = pl.BlockSpec(memory_space=pl.ANY)
    grid_spec = pltpu.PrefetchScalarGridSpec(
        num_scalar_prefetch=1,
        grid=(b, steps),
        in_specs=[tok, tok, tok, hbm, hbm,
                  pl.BlockSpec(lam_vecs.shape, lambda bi, s, pt: (0, 0)),
                  pl.BlockSpec((1, HEAD_W), lambda bi, s, pt: (0, 0))],
        out_specs=tok,
        scratch_shapes=[pltpu.VMEM((PAGE_SLOTS, n_group, page, heads, HEAD_W), cache_k.dtype),
                        pltpu.VMEM((PAGE_SLOTS, n_group, page, heads, HEAD_W), cache_v.dtype),
                        pltpu.SemaphoreType.DMA((PAGE_SLOTS, 2, n_group)),
                        pltpu.VMEM((rows, HEAD_W), BF16), pltpu.VMEM((rows, page * heads), F32),
                        pltpu.VMEM((rows, 1), F32), pltpu.VMEM((rows, 1), F32),
                        pltpu.VMEM((rows, HEAD_W), F32)],
    )
    return pl.pallas_call(
        functools.partial(_attn_sample_kernel, lam_init=lam_init),
        out_shape=jax.ShapeDtypeStruct((b, t_new, heads, HEAD_W), F32),
        grid_spec=grid_spec,
        compiler_params=_params("arbitrary", "arbitrary"),
        name="attn_sample",
    )(page_table.reshape(-1), q, k_new, v_new, cache_k, cache_v, lam_vecs, g_sub)


def _row(v):
    return v.reshape(1, -1).astype(F32)


def _post_mixer(h, p, i, W, tm):
    h = _ffn(h, _row(W['g_ffn_pre'][i]), W['w_ffn_gate'], W['w_ffn_up'], W['w_ffn_down'],
             _row(W['g_ffn_post'][i]), layer=i, tm=tm, tf=512)
    return _ple(h, p[i], W['w_ple_gate'][i], W['w_ple_proj'][i], _row(W['g_ple_post'][i]), tm=tm)


def _conv_layer(x, conv_fn, W, tm):
    glu = _pw1_glu(x, _row(W['g_mix_pre'][0]), W['w_pw1'][0], _row(W['b_pw1'][0]),
                   tm=ROW_TILE_STREAMED, tn=512)
    act, extra = conv_fn(glu)
    h = _mm_norm_res(act, W['w_pw2'][0], _row(W['b_pw2'][0]), _row(W['g_mix_post'][0]), x, tm=tm)
    return h, extra


def kernel(x_prompt, x_sample, p_prompt, p_sample, state_conv, cache_k, cache_v, page_table, g_mix_pre, g_mix_post, g_ffn_pre, g_ffn_post, g_ple_post, w_pw1, b_pw1, w_dw, b_dw, g_conv_ln, b_conv_ln, w_pw2, b_pw2, g_kv_norm, w_k, w_v, w_q, lam_q1, lam_k1, lam_q2, lam_k2, g_subln, w_o, w_ffn_gate, w_ffn_up, w_ffn_down, w_ple_proj, w_ple_gate):
    B, S, D = x_prompt.shape
    DB, T, _ = x_sample.shape
    heads = D // HEAD_W
    page = cache_k.shape[1]
    assert state_conv.shape[0] == 1 and w_q.shape[0] == 1, "one conv layer then one attention layer"

    W = {
        'g_mix_pre': g_mix_pre, 'g_mix_post': g_mix_post, 'g_ffn_pre': g_ffn_pre,
        'g_ffn_post': g_ffn_post, 'g_ple_post': g_ple_post, 'b_pw1': b_pw1, 'b_pw2': b_pw2,
    }
    for name, w in (('w_pw1', w_pw1), ('w_pw2', w_pw2), ('w_ple_proj', w_ple_proj), ('w_ple_gate', w_ple_gate)):
        W[name] = [w[i].astype(BF16) for i in range(w.shape[0])]
    for name, w in (('w_ffn_gate', w_ffn_gate), ('w_ffn_up', w_ffn_up), ('w_ffn_down', w_ffn_down)):
        W[name] = w.astype(BF16)
    wq, wk, wv, wo = w_q[0].astype(BF16), w_k.astype(BF16), w_v.astype(BF16), w_o[0].astype(BF16)
    w_dw0, b_dw0 = w_dw[0], _row(b_dw[0])
    g_ln0, b_ln0 = _row(g_conv_ln[0]), _row(b_conv_ln[0])
    lam_vecs = jnp.concatenate([lam_q1, lam_k1, lam_q2, lam_k2], axis=0).astype(F32)
    lam_init = 0.8 - 0.6 * math.exp(-0.3 * 1)
    g_sub = _row(g_subln[0])

    def attention_layer(h, pos, attn_fn, tm, q_dtype):
        cos, sin_signed = _rope_tables(pos)
        q, k, v = _qkv(h, _row(g_mix_pre[1]), _row(g_kv_norm), wq, wk, wv, cos, sin_signed,
                       tm=tm, tn=512, q_dtype=q_dtype)
        o = attn_fn(q, k, v)
        h = _mm_norm_res(o, wo, None, _row(g_mix_post[1]), h, tm=tm)
        return h, k, v

    tm_p = 512
    xp = x_prompt.reshape(B * S, D)
    pp = p_prompt.reshape(p_prompt.shape[0], B * S, -1)

    def conv_p(glu):
        act = _conv_prompt(glu, w_dw0, b_dw0, g_ln0, b_ln0, seq=S, tm=tm_p)
        return act, glu.reshape(B, S, D)[:, S - CONV_HIST:, :][None]

    h, conv_state_p = _conv_layer(xp, conv_p, W, tm_p)
    h = _post_mixer(h, pp, 0, W, tm_p)
    h, k_p, v_p = attention_layer(
        h, jnp.tile(jnp.arange(S), B),
        lambda q, k, v: _attn_prompt(q, k, v, lam_vecs, g_sub, batch=B, seq=S, tq=512, lam_init=lam_init),
        tm_p, BF16)
    y_p = _post_mixer(h, pp, 1, W, tm_p).reshape(B, S, D)

    tm_s = 512
    xs = x_sample.reshape(DB * T, D)
    ps = p_sample.reshape(p_sample.shape[0], DB * T, -1)
    past_len = page_table.shape[1] * page

    def conv_s(glu):
        return _conv_sample(state_conv[0], glu, w_dw0, b_dw0, g_ln0, b_ln0, nb=8)

    def attn_s(q, k, v):
        per_head = (DB, T, heads, HEAD_W)
        o = _attn_sample(q.reshape(per_head), k.reshape(per_head), v.reshape(per_head),
                         cache_k, cache_v, page_table, lam_vecs, g_sub, lam_init=lam_init)
        return o.reshape(DB * T, D)

    h, conv_state_s = _conv_layer(xs, conv_s, W, tm_s)
    h = _post_mixer(h, ps, 0, W, tm_s)
    h, k_s, v_s = attention_layer(h, jnp.tile(past_len + jnp.arange(T), DB), attn_s, tm_s, F32)
    y_s = _post_mixer(h, ps, 1, W, tm_s).reshape(DB, T, D)

    return (y_p, y_s, conv_state_p,
            k_p.reshape(B, S, heads, HEAD_W), v_p.reshape(B, S, heads, HEAD_W),
            conv_state_s[None],
            k_s.reshape(DB, T, heads, HEAD_W), v_s.reshape(DB, T, heads, HEAD_W))
```

```python
import functools
import math

import jax
import jax.numpy as jnp
from jax import lax
from jax.experimental import pallas as pl
from jax.experimental.pallas import tpu as pltpu

F32 = jnp.float32
BF16 = jnp.bfloat16

RMS_EPS = 1e-6
LN_EPS = 1e-5
ROPE_THETA = 10000.0
HEAD_DIM = 64
HEAD_W = 2 * HEAD_DIM
ROPE_HALF = HEAD_DIM // 2
SCORE_SCALE = HEAD_DIM ** -0.5 * math.log2(math.e)
CONV_W = 31
CONV_HIST = CONV_W - 1
CONV_HALO = 32
SUBLANES = 8
LANES = 128
V7X_VMEM_LIMIT = 56 * 1024 * 1024
ROW_TILE_STREAMED = 1024


def _params(*sem):
    return pltpu.CompilerParams(dimension_semantics=sem, vmem_limit_bytes=V7X_VMEM_LIMIT)


def _sigmoid(x):
    return 1.0 / (1.0 + jnp.exp(-x))


def _rms_scale(x):
    return x * lax.rsqrt(jnp.mean(x * x, axis=-1, keepdims=True) + RMS_EPS)


def _tile(m, want):
    t = min(m, want)
    assert m % t == 0, (m, t)
    return t


def _pw1_glu_kernel(x_ref, g_ref, wa_ref, wg_ref, ba_ref, bg_ref, o_ref, u_ref):
    @pl.when(pl.program_id(1) == 0)
    def _():
        u_ref[...] = (_rms_scale(x_ref[...]) * g_ref[...]).astype(BF16)

    u = u_ref[...]
    a = jnp.dot(u, wa_ref[...], preferred_element_type=F32) + ba_ref[...]
    gt = jnp.dot(u, wg_ref[...], preferred_element_type=F32) + bg_ref[...]
    o_ref[...] = a * _sigmoid(gt)


def _pw1_glu(x, g, w, b, *, tm, tn):
    m, d = x.shape
    tm, tn = _tile(m, tm), _tile(d, tn)
    nj = d // tn
    return pl.pallas_call(
        _pw1_glu_kernel,
        out_shape=jax.ShapeDtypeStruct((m, d), F32),
        grid=(m // tm, nj),
        in_specs=[
            pl.BlockSpec((tm, d), lambda i, j: (i, 0)),
            pl.BlockSpec((1, d), lambda i, j: (0, 0)),
            pl.BlockSpec((d, tn), lambda i, j: (0, j)),
            pl.BlockSpec((d, tn), lambda i, j: (0, j + nj)),
            pl.BlockSpec((1, tn), lambda i, j: (0, j)),
            pl.BlockSpec((1, tn), lambda i, j: (0, j + nj)),
        ],
        out_specs=pl.BlockSpec((tm, tn), lambda i, j: (i, j)),
        scratch_shapes=[pltpu.VMEM((tm, d), BF16)],
        compiler_params=_params("arbitrary", "arbitrary"),
        name="pw1_glu",
    )(x, g, w, w, b, b)


def _ln_silu(c, g, b):
    mu = jnp.mean(c, axis=-1, keepdims=True)
    xc = c - mu
    y = xc * lax.rsqrt(jnp.mean(xc * xc, axis=-1, keepdims=True) + LN_EPS) * g + b
    return y * _sigmoid(y)


CONV_ROWS_PER_ITER = 8 * SUBLANES
LN_ROWS_PER_ITER = 8 * SUBLANES


def _conv_prompt_kernel(cur_ref, prev_ref, w_ref, bdw_ref, gln_ref, bln_ref, o_ref, buf_ref, c_ref,
                        *, tiles_per_seq):
    tm, d = cur_ref.shape
    first = (pl.program_id(0) % tiles_per_seq) == 0
    lead = CONV_HALO - CONV_HIST
    groups = CONV_ROWS_PER_ITER // SUBLANES

    for lb in range(d // LANES):
        lanes = slice(lb * LANES, (lb + 1) * LANES)
        buf_ref[lb, 0:CONV_HALO, :] = jnp.where(first, 0.0, prev_ref[:, lanes])
        buf_ref[lb, CONV_HALO:, :] = cur_ref[:, lanes]
        taps = [jnp.broadcast_to(w_ref[j:j + 1, lanes], (SUBLANES, LANES)) for j in range(CONV_W)]
        bias = jnp.broadcast_to(bdw_ref[:, lanes], (SUBLANES, LANES))

        def body(r, carry, lb=lb, taps=taps, bias=bias):
            r0 = pl.multiple_of(r * CONV_ROWS_PER_ITER, CONV_ROWS_PER_ITER)
            for q in range(groups):
                base = r0 + (q // 2) * 2 * SUBLANES + (q % 2)
                acc = bias
                for j in range(CONV_W):
                    acc = acc + taps[j] * buf_ref[lb, pl.ds(base + lead + j, SUBLANES, stride=2), :]
                c_ref[lb, pl.ds(base, SUBLANES, stride=2), :] = acc
            return carry

        lax.fori_loop(0, tm // CONV_ROWS_PER_ITER, body, 0)

    def ln_body(r, carry):
        r0 = pl.multiple_of(r * LN_ROWS_PER_ITER, LN_ROWS_PER_ITER)
        c = c_ref[:, pl.ds(r0, LN_ROWS_PER_ITER), :]
        mu = jnp.sum(jnp.sum(c, axis=0), axis=-1, keepdims=True) / d
        xc = c - mu[None]
        var = jnp.sum(jnp.sum(xc * xc, axis=0), axis=-1, keepdims=True) / d
        y = xc * lax.rsqrt(var + LN_EPS)[None] * gln_ref[...] + bln_ref[...]
        act = (y * _sigmoid(y)).astype(BF16)
        for lb in range(d // LANES):
            o_ref[pl.ds(r0, LN_ROWS_PER_ITER), lb * LANES:(lb + 1) * LANES] = act[lb]
        return carry

    lax.fori_loop(0, tm // LN_ROWS_PER_ITER, ln_body, 0)


def _conv_prompt(glu, w_dw, b_dw, g_ln, b_ln, *, seq, tm):
    m, d = glu.shape
    tm = _tile(seq, tm)
    halo_blocks = tm // CONV_HALO
    nlb = d // LANES
    chan = pl.BlockSpec((nlb, 1, LANES), lambda i: (0, 0, 0))
    return pl.pallas_call(
        functools.partial(_conv_prompt_kernel, tiles_per_seq=seq // tm),
        out_shape=jax.ShapeDtypeStruct((m, d), BF16),
        grid=(m // tm,),
        in_specs=[
            pl.BlockSpec((tm, d), lambda i: (i, 0)),
            pl.BlockSpec((CONV_HALO, d), lambda i: (jnp.maximum(i * halo_blocks - 1, 0), 0)),
            pl.BlockSpec((CONV_W, d), lambda i: (0, 0)),
            pl.BlockSpec((1, d), lambda i: (0, 0)),
            chan,
            chan,
        ],
        out_specs=pl.BlockSpec((tm, d), lambda i: (i, 0)),
        scratch_shapes=[pltpu.VMEM((nlb, CONV_HALO + tm, LANES), F32), pltpu.VMEM((nlb, tm, LANES), F32)],
        compiler_params=_params("arbitrary"),
        name="conv_prompt",
    )(glu, glu, w_dw, b_dw, g_ln.reshape(nlb, 1, LANES), b_ln.reshape(nlb, 1, LANES))


def _conv_sample_kernel(st_ref, glu_ref, w_ref, bdw_ref, gln_ref, bln_ref, o_ref, ns_ref, pad_ref, c_ref,
                        *, t_new):
    nb = st_ref.shape[0]
    w = w_ref[...]
    for bi in range(nb):
        new = glu_ref[bi * t_new:(bi + 1) * t_new, :]
        pad_ref[0:CONV_HIST, :] = st_ref[bi]
        pad_ref[CONV_HIST:CONV_HIST + t_new, :] = new
        for t in range(t_new):
            c_ref[bi * t_new + t:bi * t_new + t + 1, :] = (
                jnp.sum(pad_ref[t:t + CONV_W, :] * w, axis=0, keepdims=True) + bdw_ref[...])
        ns_ref[bi] = pad_ref[t_new:t_new + CONV_HIST, :]
    o_ref[...] = _ln_silu(c_ref[...], gln_ref[...], bln_ref[...]).astype(BF16)


def _conv_sample(state, glu, w_dw, b_dw, g_ln, b_ln, *, nb):
    b, hist, d = state.shape
    t_new = glu.shape[0] // b
    nb = _tile(b, nb)
    rows = nb * t_new
    return pl.pallas_call(
        functools.partial(_conv_sample_kernel, t_new=t_new),
        out_shape=(jax.ShapeDtypeStruct((b * t_new, d), BF16), jax.ShapeDtypeStruct((b, hist, d), F32)),
        grid=(b // nb,),
        in_specs=[
            pl.BlockSpec((nb, hist, d), lambda i: (i, 0, 0)),
            pl.BlockSpec((rows, d), lambda i: (i, 0)),
            pl.BlockSpec((CONV_W, d), lambda i: (0, 0)),
            pl.BlockSpec((1, d), lambda i: (0, 0)),
            pl.BlockSpec((1, d), lambda i: (0, 0)),
            pl.BlockSpec((1, d), lambda i: (0, 0)),
        ],
        out_specs=(pl.BlockSpec((rows, d), lambda i: (i, 0)),
                   pl.BlockSpec((nb, hist, d), lambda i: (i, 0, 0))),
        scratch_shapes=[pltpu.VMEM((CONV_HIST + t_new + SUBLANES, d), F32), pltpu.VMEM((rows, d), F32)],
        compiler_params=_params("arbitrary"),
        name="conv_sample",
    )(state, glu, w_dw, b_dw, g_ln, b_ln)


def _mm_norm_res_kernel(*refs, has_bias):
    if has_bias:
        a_ref, w_ref, b_ref, g_ref, r_ref, o_ref = refs
    else:
        a_ref, w_ref, g_ref, r_ref, o_ref = refs
    m = jnp.dot(a_ref[...].astype(BF16), w_ref[...], preferred_element_type=F32)
    if has_bias:
        m = m + b_ref[...]
    o_ref[...] = r_ref[...] + _rms_scale(m) * g_ref[...]


def _mm_norm_res(a, w, bias, g, res, *, tm):
    m, k = a.shape
    d = w.shape[1]
    tm = _tile(m, tm)
    row = pl.BlockSpec((1, d), lambda i: (0, 0))
    has_bias = bias is not None
    in_specs = [pl.BlockSpec((tm, k), lambda i: (i, 0)), pl.BlockSpec((k, d), lambda i: (0, 0))]
    args = [a, w]
    if has_bias:
        in_specs.append(row)
        args.append(bias)
    in_specs += [row, pl.BlockSpec((tm, d), lambda i: (i, 0))]
    args += [g, res]
    return pl.pallas_call(
        functools.partial(_mm_norm_res_kernel, has_bias=has_bias),
        out_shape=jax.ShapeDtypeStruct((m, d), F32),
        grid=(m // tm,),
        in_specs=in_specs,
        out_specs=pl.BlockSpec((tm, d), lambda i: (i, 0)),
        compiler_params=_params("arbitrary"),
        name="mm_norm_res",
    )(*args)


def _ffn_kernel(h_ref, gpre_ref, wg_ref, wu_ref, wd_ref, gpost_ref, o_ref, z_ref, acc_ref):
    c = pl.program_id(1)

    @pl.when(c == 0)
    def _():
        z_ref[...] = (_rms_scale(h_ref[...]) * gpre_ref[...]).astype(BF16)
        acc_ref[...] = jnp.zeros(acc_ref.shape, F32)

    z = z_ref[...]
    gate = jnp.dot(z, wg_ref[...], preferred_element_type=F32)
    up = jnp.dot(z, wu_ref[...], preferred_element_type=F32)
    act = (gate * _sigmoid(gate) * up).astype(BF16)
    acc_ref[...] += jnp.dot(act, wd_ref[...], preferred_element_type=F32)

    @pl.when(c == pl.num_programs(1) - 1)
    def _():
        o_ref[...] = h_ref[...] + _rms_scale(acc_ref[...]) * gpost_ref[...]


def _ffn(h, g_pre, wg, wu, wd, g_post, *, layer, tm, tf):
    m, d = h.shape
    f = wg.shape[2]
    tm, tf = _tile(m, tm), _tile(f, tf)
    row = pl.BlockSpec((1, d), lambda i, c: (0, 0))
    return pl.pallas_call(
        _ffn_kernel,
        out_shape=jax.ShapeDtypeStruct((m, d), F32),
        grid=(m // tm, f // tf),
        in_specs=[
            pl.BlockSpec((tm, d), lambda i, c: (i, 0)),
            row,
            pl.BlockSpec((None, d, tf), lambda i, c: (layer, 0, c)),
            pl.BlockSpec((None, d, tf), lambda i, c: (layer, 0, c)),
            pl.BlockSpec((None, tf, d), lambda i, c: (layer, c, 0)),
            row,
        ],
        out_specs=pl.BlockSpec((tm, d), lambda i, c: (i, 0)),
        scratch_shapes=[pltpu.VMEM((tm, d), BF16), pltpu.VMEM((tm, d), F32)],
        compiler_params=_params("arbitrary", "arbitrary"),
        name="ffn",
    )(h, g_pre, wg, wu, wd, g_post)


def _ple_kernel(h_ref, p_ref, wgate_ref, wproj_ref, g_ref, o_ref):
    h = h_ref[...]
    gate = _sigmoid(jnp.dot(h.astype(BF16), wgate_ref[...], preferred_element_type=F32))
    e = jnp.dot(p_ref[...].astype(BF16), wproj_ref[...], preferred_element_type=F32)
    o_ref[...] = h + _rms_scale(gate * e) * g_ref[...]


def _ple(h, p, w_gate, w_proj, g, *, tm):
    m, d = h.shape
    pd = p.shape[1]
    tm = _tile(m, tm)
    return pl.pallas_call(
        _ple_kernel,
        out_shape=jax.ShapeDtypeStruct((m, d), F32),
        grid=(m // tm,),
        in_specs=[
            pl.BlockSpec((tm, d), lambda i: (i, 0)),
            pl.BlockSpec((tm, pd), lambda i: (i, 0)),
            pl.BlockSpec((d, d), lambda i: (0, 0)),
            pl.BlockSpec((pd, d), lambda i: (0, 0)),
            pl.BlockSpec((1, d), lambda i: (0, 0)),
        ],
        out_specs=pl.BlockSpec((tm, d), lambda i: (i, 0)),
        compiler_params=_params("arbitrary"),
        name="ple",
    )(h, p, w_gate, w_proj, g)


def _rope(y, cos, sin_signed):
    lane = lax.broadcasted_iota(jnp.int32, (y.shape[0], LANES), 1)
    low_half = (lane % HEAD_DIM) < ROPE_HALF
    out = []
    for lb in range(y.shape[1] // LANES):
        yb = y[:, lb * LANES:(lb + 1) * LANES]
        partner = jnp.where(low_half, pltpu.roll(yb, LANES - ROPE_HALF, 1), pltpu.roll(yb, ROPE_HALF, 1))
        out.append(yb * cos + partner * sin_signed)
    return out


def _qkv_kernel(h_ref, gu_ref, gs_ref, wq_ref, wk_ref, wv_ref, cos_ref, sin_ref,
                q_ref, k_ref, v_ref, u_ref, s_ref):
    @pl.when(pl.program_id(1) == 0)
    def _():
        r = _rms_scale(h_ref[...])
        u_ref[...] = (r * gu_ref[...]).astype(BF16)
        s_ref[...] = (r * gs_ref[...]).astype(BF16)

    cos, sin_signed = cos_ref[...], sin_ref[...]
    s = s_ref[...]
    q = _rope(jnp.dot(u_ref[...], wq_ref[...], preferred_element_type=F32), cos, sin_signed)
    k = _rope(jnp.dot(s, wk_ref[...], preferred_element_type=F32), cos, sin_signed)
    for lb in range(len(q)):
        lanes = slice(lb * LANES, (lb + 1) * LANES)
        q_ref[:, lanes] = (q[lb] * SCORE_SCALE).astype(q_ref.dtype)
        k_ref[:, lanes] = k[lb]
    v_ref[...] = jnp.dot(s, wv_ref[...], preferred_element_type=F32)


def _qkv(h, g_u, g_s, wq, wk, wv, cos, sin_signed, *, tm, tn, q_dtype):
    m, d = h.shape
    tm, tn = _tile(m, tm), _tile(d, tn)
    row = pl.BlockSpec((1, d), lambda i, j: (0, 0))
    wcol = pl.BlockSpec((d, tn), lambda i, j: (0, j))
    tab = pl.BlockSpec((tm, LANES), lambda i, j: (i, 0))
    out = pl.BlockSpec((tm, tn), lambda i, j: (i, j))
    return pl.pallas_call(
        _qkv_kernel,
        out_shape=(jax.ShapeDtypeStruct((m, d), q_dtype), jax.ShapeDtypeStruct((m, d), F32),
                   jax.ShapeDtypeStruct((m, d), F32)),
        grid=(m // tm, d // tn),
        in_specs=[pl.BlockSpec((tm, d), lambda i, j: (i, 0)), row, row, wcol, wcol, wcol, tab, tab],
        out_specs=(out, out, out),
        scratch_shapes=[pltpu.VMEM((tm, d), BF16), pltpu.VMEM((tm, d), BF16)],
        compiler_params=_params("arbitrary", "arbitrary"),
        name="qkv_rope",
    )(h, g_u, g_s, wq, wk, wv, cos, sin_signed)


def _rope_tables(pos):
    inv = 1.0 / (ROPE_THETA ** (jnp.arange(ROPE_HALF, dtype=F32) / ROPE_HALF))
    ang = pos.astype(F32)[:, None] * inv[None, :]
    cos, sin = jnp.cos(ang), jnp.sin(ang)
    reps = LANES // HEAD_DIM
    cos_l = jnp.tile(jnp.concatenate([cos, cos], axis=-1), (1, reps))
    sin_l = jnp.tile(jnp.concatenate([-sin, sin], axis=-1), (1, reps))
    return cos_l, sin_l


def _lambda(lam_ref, lam_init):
    lam = lam_ref[...]
    d1 = jnp.sum(lam[0:1] * lam[1:2], axis=-1, keepdims=True)
    d2 = jnp.sum(lam[2:3] * lam[3:4], axis=-1, keepdims=True)
    return jnp.exp(d1) - jnp.exp(d2) + lam_init


ROW_BLOCKS = 8


def _attn_prompt_kernel(q_ref, k_ref, v_ref, lam_ref, gsub_ref, o_ref, kb_ref, vb_ref, *, lam_init):
    tq = q_ref.shape[0]
    i = pl.program_id(2)
    nt = (((1,), (1,)), ((), ()))

    @pl.when(i == 0)
    def _():
        kb_ref[...] = k_ref[...].astype(BF16)
        vb_ref[:, 0:HEAD_W] = v_ref[...].astype(BF16)
        vb_ref[:, HEAD_W:] = jnp.ones((vb_ref.shape[0], HEAD_W), BF16)

    def attend(tile):
        past = tile * tq
        tr = tq // ROW_BLOCKS
        row = lax.broadcasted_iota(jnp.int32, (tr, tr), 0)
        col = lax.broadcasted_iota(jnp.int32, (tr, tr), 1)
        lam = _lambda(lam_ref, lam_init)
        for rb in range(ROW_BLOCKS):
            before = past + rb * tr
            q = q_ref[rb * tr:(rb + 1) * tr, :]
            lane = lax.broadcasted_iota(jnp.int32, q.shape, 1)
            zero = jnp.zeros_like(q)
            out = []
            for c in range(2):
                qc = jnp.where((lane // HEAD_DIM) == c, q, zero)
                s_diag = lax.dot_general(qc, kb_ref[before:before + tr, :], nt, preferred_element_type=F32)
                s_diag = jnp.where(col <= row, s_diag, -jnp.inf)
                m = jnp.max(s_diag, axis=-1, keepdims=True)
                if before:
                    s_past = lax.dot_general(qc, kb_ref[0:before, :], nt, preferred_element_type=F32)
                    m = jnp.maximum(m, jnp.max(s_past, axis=-1, keepdims=True))
                acc = jnp.dot(jnp.exp2(s_diag - m).astype(BF16), vb_ref[before:before + tr, :],
                              preferred_element_type=F32)
                if before:
                    acc = acc + jnp.dot(jnp.exp2(s_past - m).astype(BF16), vb_ref[0:before, :],
                                        preferred_element_type=F32)
                out.append(acc[:, 0:HEAD_W] / acc[:, HEAD_W:])
            o = out[0] - lam * out[1]
            o_ref[rb * tr:(rb + 1) * tr, :] = (
                _rms_scale(o) * gsub_ref[...] * (1.0 - lam_init)).astype(o_ref.dtype)

    for tile in range(kb_ref.shape[0] // tq):
        pl.when(i == tile)(functools.partial(attend, tile))


def _attn_prompt(q, k, v, lam_vecs, g_sub, *, batch, seq, tq, lam_init):
    m, d = q.shape
    heads = d // HEAD_W
    tq = _tile(seq, tq)
    nq = seq // tq
    return pl.pallas_call(
        functools.partial(_attn_prompt_kernel, lam_init=lam_init),
        out_shape=jax.ShapeDtypeStruct((m, d), BF16),
        grid=(batch, heads, nq),
        in_specs=[
            pl.BlockSpec((tq, HEAD_W), lambda b, h, i: (b * nq + i, h)),
            pl.BlockSpec((seq, HEAD_W), lambda b, h, i: (b, h)),
            pl.BlockSpec((seq, HEAD_W), lambda b, h, i: (b, h)),
            pl.BlockSpec(lam_vecs.shape, lambda b, h, i: (0, 0)),
            pl.BlockSpec((1, HEAD_W), lambda b, h, i: (0, 0)),
        ],
        out_specs=pl.BlockSpec((tq, HEAD_W), lambda b, h, i: (b * nq + i, h)),
        scratch_shapes=[pltpu.VMEM((seq, HEAD_W), BF16), pltpu.VMEM((seq, 2 * HEAD_W), BF16)],
        compiler_params=_params("arbitrary", "arbitrary", "arbitrary"),
        name="attn_prompt",
    )(q, k, v, lam_vecs, g_sub)


PAGES_PER_STEP = 4
PAGE_SLOTS = 3


def _attn_sample_kernel(pt_ref, q_ref, kn_ref, vn_ref, ck_hbm, cv_hbm, lam_ref, gsub_ref, o_ref,
                        kbuf_ref, vbuf_ref, sem_ref, qrows_ref, bias_ref, m_ref, l_ref, acc_ref,
                        *, lam_init):
    t_new, heads = q_ref.shape[1], q_ref.shape[2]
    rows = 2 * t_new * heads
    n_group, page = kbuf_ref.shape[1], kbuf_ref.shape[2]
    step = pl.program_id(1)
    nt = (((1,), (1,)), ((), ()))

    n_steps = pl.num_programs(0) * pl.num_programs(1)
    cur = pl.program_id(0) * pl.num_programs(1) + step

    def page_copies(group):
        slot = group % PAGE_SLOTS
        copies = []
        for g in range(n_group):
            pg = pt_ref[group * n_group + g]
            copies.append(pltpu.make_async_copy(ck_hbm.at[pg], kbuf_ref.at[slot, g], sem_ref.at[slot, 0, g]))
            copies.append(pltpu.make_async_copy(cv_hbm.at[pg], vbuf_ref.at[slot, g], sem_ref.at[slot, 1, g]))
        return copies

    def start(group):
        for c in page_copies(group):
            c.start()

    @pl.when(cur == 0)
    def _():
        for group in range(PAGE_SLOTS - 1):
            start(group)

    @pl.when(cur + (PAGE_SLOTS - 1) < n_steps)
    def _():
        start(cur + (PAGE_SLOTS - 1))

    for c in page_copies(cur):
        c.wait()
    slot = cur % PAGE_SLOTS

    @pl.when(step == 0)
    def _():
        lane = lax.broadcasted_iota(jnp.int32, (heads, HEAD_W), 1)
        for c in range(2):
            own = (lane // HEAD_DIM) == c
            for t in range(t_new):
                r0 = (c * t_new + t) * heads
                qrows_ref[r0:r0 + heads, :] = jnp.where(own, q_ref[0, t], 0.0).astype(BF16)
        row_head = lax.broadcasted_iota(jnp.int32, bias_ref.shape, 0) % heads
        key_head = lax.broadcasted_iota(jnp.int32, bias_ref.shape, 1) % heads
        bias_ref[...] = jnp.where(row_head == key_head, 0.0, -jnp.inf)
        m_ref[...] = jnp.full(m_ref.shape, -jnp.inf, F32)
        l_ref[...] = jnp.zeros(l_ref.shape, F32)
        acc_ref[...] = jnp.zeros(acc_ref.shape, F32)

    def partial_softmax(s, v):
        m = jnp.max(s, axis=-1, keepdims=True)
        p = jnp.exp2(s - m)
        return m, jnp.sum(p, axis=-1, keepdims=True), jnp.dot(p.astype(BF16), v, preferred_element_type=F32)

    def merge(parts):
        m_prev = m_ref[...]
        m_new = m_prev
        for m, _, _ in parts:
            m_new = jnp.maximum(m_new, m)
        alpha = jnp.exp2(m_prev - m_new)
        l_new = alpha * l_ref[...]
        acc = alpha * acc_ref[...]
        for m, l, a in parts:
            w = jnp.exp2(m - m_new)
            l_new = l_new + w * l
            acc = acc + w * a
        m_ref[...], l_ref[...], acc_ref[...] = m_new, l_new, acc

    q_rows = qrows_ref[...]
    bias = bias_ref[...]
    scores = []
    for g in range(n_group):
        keys = kbuf_ref[slot, g].reshape(page * heads, HEAD_W).astype(BF16)
        scores.append(lax.dot_general(q_rows, keys, nt, preferred_element_type=F32) + bias)
    maxes = [jnp.max(s, axis=-1, keepdims=True) for s in scores]
    probs = [jnp.exp2(s - m) for s, m in zip(scores, maxes)]
    sums = [jnp.sum(p, axis=-1, keepdims=True) for p in probs]
    outs = [jnp.dot(p.astype(BF16), vbuf_ref[slot, g].reshape(page * heads, HEAD_W).astype(BF16),
                    preferred_element_type=F32) for g, p in enumerate(probs)]
    merge(list(zip(maxes, sums, outs)))

    @pl.when(step == pl.num_programs(1) - 1)
    def _():
        keys = kn_ref[0].reshape(t_new * heads, HEAD_W).astype(BF16)
        s = lax.dot_general(q_rows, keys, nt, preferred_element_type=F32)
        row = lax.broadcasted_iota(jnp.int32, s.shape, 0)
        col = lax.broadcasted_iota(jnp.int32, s.shape, 1)
        ok = (row % heads == col % heads) & (col // heads <= (row // heads) % t_new)
        merge([partial_softmax(jnp.where(ok, s, -jnp.inf),
                               vn_ref[0].reshape(t_new * heads, HEAD_W).astype(BF16))])

        half = rows // 2
        lam = _lambda(lam_ref, lam_init)
        o = acc_ref[...] / l_ref[...]
        w = o[0:half] - lam * o[half:rows]
        w = _rms_scale(w) * gsub_ref[...] * (1.0 - lam_init)
        o_ref[0] = w.reshape(t_new, heads, HEAD_W)


def _attn_sample(q, k_new, v_new, cache_k, cache_v, page_table, lam_vecs, g_sub, *, lam_init):
    b, t_new, heads, _ = q.shape
    n_pages = page_table.shape[1]
    page = cache_k.shape[1]
    rows = 2 * t_new * heads
    n_group = math.gcd(n_pages, PAGES_PER_STEP)
    steps = n_pages // n_group
    assert b * steps >= PAGE_SLOTS
    tok = pl.BlockSpec((1, t_new, heads, HEAD_W), lambda bi, s, pt: (bi, 0, 0, 0))
    hbm = pl.BlockSpec(memory_space=pl.ANY)
    grid_spec = pltpu.PrefetchScalarGridSpec(
        num_scalar_prefetch=1,
        grid=(b, steps),
        in_specs=[tok, tok, tok, hbm, hbm,
                  pl.BlockSpec(lam_vecs.shape, lambda bi, s, pt: (0, 0)),
                  pl.BlockSpec((1, HEAD_W), lambda bi, s, pt: (0, 0))],
        out_specs=tok,
        scratch_shapes=[pltpu.VMEM((PAGE_SLOTS, n_group, page, heads, HEAD_W), cache_k.dtype),
                        pltpu.VMEM((PAGE_SLOTS, n_group, page, heads, HEAD_W), cache_v.dtype),
                        pltpu.SemaphoreType.DMA((PAGE_SLOTS, 2, n_group)),
                        pltpu.VMEM((rows, HEAD_W), BF16), pltpu.VMEM((rows, page * heads), F32),
                        pltpu.VMEM((rows, 1), F32), pltpu.VMEM((rows, 1), F32),
                        pltpu.VMEM((rows, HEAD_W), F32)],
    )
    return pl.pallas_call(
        functools.partial(_attn_sample_kernel, lam_init=lam_init),
        out_shape=jax.ShapeDtypeStruct((b, t_new, heads, HEAD_W), F32),
        grid_spec=grid_spec,
        compiler_params=_params("arbitrary", "arbitrary"),
        name="attn_sample",
    )(page_table.reshape(-1), q, k_new, v_new, cache_k, cache_v, lam_vecs, g_sub)


def _row(v):
    return v.reshape(1, -1).astype(F32)


def _post_mixer(h, p, i, W, tm):
    h = _ffn(h, _row(W['g_ffn_pre'][i]), W['w_ffn_gate'], W['w_ffn_up'], W['w_ffn_down'],
             _row(W['g_ffn_post'][i]), layer=i, tm=tm, tf=512)
    return _ple(h, p[i], W['w_ple_gate'][i], W['w_ple_proj'][i], _row(W['g_ple_post'][i]), tm=tm)


def _conv_layer(x, conv_fn, W, tm):
    glu = _pw1_glu(x, _row(W['g_mix_pre'][0]), W['w_pw1'][0], _row(W['b_pw1'][0]),
                   tm=ROW_TILE_STREAMED, tn=512)
    act, extra = conv_fn(glu)
    h = _mm_norm_res(act, W['w_pw2'][0], _row(W['b_pw2'][0]), _row(W['g_mix_post'][0]), x, tm=tm)
    return h, extra


def kernel(x_prompt, x_sample, p_prompt, p_sample, state_conv, cache_k, cache_v, page_table, g_mix_pre, g_mix_post, g_ffn_pre, g_ffn_post, g_ple_post, w_pw1, b_pw1, w_dw, b_dw, g_conv_ln, b_conv_ln, w_pw2, b_pw2, g_kv_norm, w_k, w_v, w_q, lam_q1, lam_k1, lam_q2, lam_k2, g_subln, w_o, w_ffn_gate, w_ffn_up, w_ffn_down, w_ple_proj, w_ple_gate):
    B, S, D = x_prompt.shape
    DB, T, _ = x_sample.shape
    heads = D // HEAD_W
    page = cache_k.shape[1]
    assert state_conv.shape[0] == 1 and w_q.shape[0] == 1, "one conv layer then one attention layer"

    W = {
        'g_mix_pre': g_mix_pre, 'g_mix_post': g_mix_post, 'g_ffn_pre': g_ffn_pre,
        'g_ffn_post': g_ffn_post, 'g_ple_post': g_ple_post, 'b_pw1': b_pw1, 'b_pw2': b_pw2,
    }
    for name, w in (('w_pw1', w_pw1), ('w_pw2', w_pw2), ('w_ple_proj', w_ple_proj), ('w_ple_gate', w_ple_gate)):
        W[name] = [w[i].astype(BF16) for i in range(w.shape[0])]
    for name, w in (('w_ffn_gate', w_ffn_gate), ('w_ffn_up', w_ffn_up), ('w_ffn_down', w_ffn_down)):
        W[name] = w.astype(BF16)
    wq, wk, wv, wo = w_q[0].astype(BF16), w_k.astype(BF16), w_v.astype(BF16), w_o[0].astype(BF16)
    w_dw0, b_dw0 = w_dw[0], _row(b_dw[0])
    g_ln0, b_ln0 = _row(g_conv_ln[0]), _row(b_conv_ln[0])
    lam_vecs = jnp.concatenate([lam_q1, lam_k1, lam_q2, lam_k2], axis=0).astype(F32)
    lam_init = 0.8 - 0.6 * math.exp(-0.3 * 1)
    g_sub = _row(g_subln[0])

    def attention_layer(h, pos, attn_fn, tm, q_dtype):
        cos, sin_signed = _rope_tables(pos)
        q, k, v = _qkv(h, _row(g_mix_pre[1]), _row(g_kv_norm), wq, wk, wv, cos, sin_signed,
                       tm=tm, tn=512, q_dtype=q_dtype)
        o = attn_fn(q, k, v)
        h = _mm_norm_res(o, wo, None, _row(g_mix_post[1]), h, tm=tm)
        return h, k, v

    tm_p = 512
    xp = x_prompt.reshape(B * S, D)
    pp = p_prompt.reshape(p_prompt.shape[0], B * S, -1)

    def conv_p(glu):
        act = _conv_prompt(glu, w_dw0, b_dw0, g_ln0, b_ln0, seq=S, tm=tm_p)
        return act, glu.reshape(B, S, D)[:, S - CONV_HIST:, :][None]

    h, conv_state_p = _conv_layer(xp, conv_p, W, tm_p)
    h = _post_mixer(h, pp, 0, W, tm_p)
    h, k_p, v_p = attention_layer(
        h, jnp.tile(jnp.arange(S), B),
        lambda q, k, v: _attn_prompt(q, k, v, lam_vecs, g_sub, batch=B, seq=S, tq=2048, lam_init=lam_init),
        tm_p, BF16)
    y_p = _post_mixer(h, pp, 1, W, tm_p).reshape(B, S, D)

    tm_s = 512
    xs = x_sample.reshape(DB * T, D)
    ps = p_sample.reshape(p_sample.shape[0], DB * T, -1)
    past_len = page_table.shape[1] * page

    def conv_s(glu):
        return _conv_sample(state_conv[0], glu, w_dw0, b_dw0, g_ln0, b_ln0, nb=8)

    def attn_s(q, k, v):
        per_head = (DB, T, heads, HEAD_W)
        o = _attn_sample(q.reshape(per_head), k.reshape(per_head), v.reshape(per_head),
                         cache_k, cache_v, page_table, lam_vecs, g_sub, lam_init=lam_init)
        return o.reshape(DB * T, D)

    h, conv_state_s = _conv_layer(xs, conv_s, W, tm_s)
    h = _post_mixer(h, ps, 0, W, tm_s)
    h, k_s, v_s = attention_layer(h, jnp.tile(past_len + jnp.arange(T), DB), attn_s, tm_s, F32)
    y_s = _post_mixer(h, ps, 1, W, tm_s).reshape(DB, T, D)

    return (y_p, y_s, conv_state_p,
            k_p.reshape(B, S, heads, HEAD_W), v_p.reshape(B, S, heads, HEAD_W),
            conv_state_s[None],
            k_s.reshape(DB, T, heads, HEAD_W), v_s.reshape(DB, T, heads, HEAD_W))
```

```python
import functools
import math

import jax
import jax.numpy as jnp
from jax import lax
from jax.experimental import pallas as pl
from jax.experimental.pallas import tpu as pltpu

F32 = jnp.float32
BF16 = jnp.bfloat16

RMS_EPS = 1e-6
LN_EPS = 1e-5
ROPE_THETA = 10000.0
HEAD_DIM = 64
HEAD_W = 2 * HEAD_DIM
ROPE_HALF = HEAD_DIM // 2
SCORE_SCALE = HEAD_DIM ** -0.5 * math.log2(math.e)
CONV_W = 31
CONV_HIST = CONV_W - 1
CONV_HALO = 32
SUBLANES = 8
LANES = 128
V7X_VMEM_LIMIT = 56 * 1024 * 1024
ROW_TILE_STREAMED = 1024


def _params(*sem):
    return pltpu.CompilerParams(dimension_semantics=sem, vmem_limit_bytes=V7X_VMEM_LIMIT)


def _sigmoid(x):
    return 1.0 / (1.0 + jnp.exp(-x))


def _rms_scale(x):
    return x * lax.rsqrt(jnp.mean(x * x, axis=-1, keepdims=True) + RMS_EPS)


def _tile(m, want):
    t = min(m, want)
    assert m % t == 0, (m, t)
    return t


def _pw1_glu_kernel(x_ref, g_ref, wa_ref, wg_ref, ba_ref, bg_ref, o_ref, u_ref):
    @pl.when(pl.program_id(1) == 0)
    def _():
        u_ref[...] = (_rms_scale(x_ref[...]) * g_ref[...]).astype(BF16)

    u = u_ref[...]
    a = jnp.dot(u, wa_ref[...], preferred_element_type=F32) + ba_ref[...]
    gt = jnp.dot(u, wg_ref[...], preferred_element_type=F32) + bg_ref[...]
    o_ref[...] = a * _sigmoid(gt)


def _pw1_glu(x, g, w, b, *, tm, tn):
    m, d = x.shape
    tm, tn = _tile(m, tm), _tile(d, tn)
    nj = d // tn
    return pl.pallas_call(
        _pw1_glu_kernel,
        out_shape=jax.ShapeDtypeStruct((m, d), F32),
        grid=(m // tm, nj),
        in_specs=[
            pl.BlockSpec((tm, d), lambda i, j: (i, 0)),
            pl.BlockSpec((1, d), lambda i, j: (0, 0)),
            pl.BlockSpec((d, tn), lambda i, j: (0, j)),
            pl.BlockSpec((d, tn), lambda i, j: (0, j + nj)),
            pl.BlockSpec((1, tn), lambda i, j: (0, j)),
            pl.BlockSpec((1, tn), lambda i, j: (0, j + nj)),
        ],
        out_specs=pl.BlockSpec((tm, tn), lambda i, j: (i, j)),
        scratch_shapes=[pltpu.VMEM((tm, d), BF16)],
        compiler_params=_params("arbitrary", "arbitrary"),
        name="pw1_glu",
    )(x, g, w, w, b, b)


def _ln_silu(c, g, b):
    mu = jnp.mean(c, axis=-1, keepdims=True)
    xc = c - mu
    y = xc * lax.rsqrt(jnp.mean(xc * xc, axis=-1, keepdims=True) + LN_EPS) * g + b
    return y * _sigmoid(y)


CONV_ROWS_PER_ITER = 8 * SUBLANES
LN_ROWS_PER_ITER = 8 * SUBLANES


def _conv_prompt_kernel(cur_ref, prev_ref, w_ref, bdw_ref, gln_ref, bln_ref, o_ref, buf_ref, c_ref,
                        *, tiles_per_seq):
    tm, d = cur_ref.shape
    first = (pl.program_id(0) % tiles_per_seq) == 0
    lead = CONV_HALO - CONV_HIST
    groups = CONV_ROWS_PER_ITER // SUBLANES

    for lb in range(d // LANES):
        lanes = slice(lb * LANES, (lb + 1) * LANES)
        buf_ref[lb, 0:CONV_HALO, :] = jnp.where(first, 0.0, prev_ref[:, lanes])
        buf_ref[lb, CONV_HALO:, :] = cur_ref[:, lanes]
        taps = [jnp.broadcast_to(w_ref[j:j + 1, lanes], (SUBLANES, LANES)) for j in range(CONV_W)]
        bias = jnp.broadcast_to(bdw_ref[:, lanes], (SUBLANES, LANES))

        def body(r, carry, lb=lb, taps=taps, bias=bias):
            r0 = pl.multiple_of(r * CONV_ROWS_PER_ITER, CONV_ROWS_PER_ITER)
            for q in range(groups):
                base = r0 + (q // 2) * 2 * SUBLANES + (q % 2)
                acc = bias
                for j in range(CONV_W):
                    acc = acc + taps[j] * buf_ref[lb, pl.ds(base + lead + j, SUBLANES, stride=2), :]
                c_ref[lb, pl.ds(base, SUBLANES, stride=2), :] = acc
            return carry

        lax.fori_loop(0, tm // CONV_ROWS_PER_ITER, body, 0)

    def ln_body(r, carry):
        r0 = pl.multiple_of(r * LN_ROWS_PER_ITER, LN_ROWS_PER_ITER)
        c = c_ref[:, pl.ds(r0, LN_ROWS_PER_ITER), :]
        mu = jnp.sum(jnp.sum(c, axis=0), axis=-1, keepdims=True) / d
        xc = c - mu[None]
        var = jnp.sum(jnp.sum(xc * xc, axis=0), axis=-1, keepdims=True) / d
        y = xc * lax.rsqrt(var + LN_EPS)[None] * gln_ref[...] + bln_ref[...]
        act = (y * _sigmoid(y)).astype(BF16)
        for lb in range(d // LANES):
            o_ref[pl.ds(r0, LN_ROWS_PER_ITER), lb * LANES:(lb + 1) * LANES] = act[lb]
        return carry

    lax.fori_loop(0, tm // LN_ROWS_PER_ITER, ln_body, 0)


def _conv_prompt(glu, w_dw, b_dw, g_ln, b_ln, *, seq, tm):
    m, d = glu.shape
    tm = _tile(seq, tm)
    halo_blocks = tm // CONV_HALO
    nlb = d // LANES
    chan = pl.BlockSpec((nlb, 1, LANES), lambda i: (0, 0, 0))
    return pl.pallas_call(
        functools.partial(_conv_prompt_kernel, tiles_per_seq=seq // tm),
        out_shape=jax.ShapeDtypeStruct((m, d), BF16),
        grid=(m // tm,),
        in_specs=[
            pl.BlockSpec((tm, d), lambda i: (i, 0)),
            pl.BlockSpec((CONV_HALO, d), lambda i: (jnp.maximum(i * halo_blocks - 1, 0), 0)),
            pl.BlockSpec((CONV_W, d), lambda i: (0, 0)),
            pl.BlockSpec((1, d), lambda i: (0, 0)),
            chan,
            chan,
        ],
        out_specs=pl.BlockSpec((tm, d), lambda i: (i, 0)),
        scratch_shapes=[pltpu.VMEM((nlb, CONV_HALO + tm, LANES), F32), pltpu.VMEM((nlb, tm, LANES), F32)],
        compiler_params=_params("arbitrary"),
        name="conv_prompt",
    )(glu, glu, w_dw, b_dw, g_ln.reshape(nlb, 1, LANES), b_ln.reshape(nlb, 1, LANES))


def _conv_sample_kernel(st_ref, glu_ref, w_ref, bdw_ref, gln_ref, bln_ref, o_ref, ns_ref, pad_ref, c_ref,
                        *, t_new):
    nb = st_ref.shape[0]
    w = w_ref[...]
    for bi in range(nb):
        new = glu_ref[bi * t_new:(bi + 1) * t_new, :]
        pad_ref[0:CONV_HIST, :] = st_ref[bi]
        pad_ref[CONV_HIST:CONV_HIST + t_new, :] = new
        for t in range(t_new):
            c_ref[bi * t_new + t:bi * t_new + t + 1, :] = (
                jnp.sum(pad_ref[t:t + CONV_W, :] * w, axis=0, keepdims=True) + bdw_ref[...])
        ns_ref[bi] = pad_ref[t_new:t_new + CONV_HIST, :]
    o_ref[...] = _ln_silu(c_ref[...], gln_ref[...], bln_ref[...]).astype(BF16)


def _conv_sample(state, glu, w_dw, b_dw, g_ln, b_ln, *, nb):
    b, hist, d = state.shape
    t_new = glu.shape[0] // b
    nb = _tile(b, nb)
    rows = nb * t_new
    return pl.pallas_call(
        functools.partial(_conv_sample_kernel, t_new=t_new),
        out_shape=(jax.ShapeDtypeStruct((b * t_new, d), BF16), jax.ShapeDtypeStruct((b, hist, d), F32)),
        grid=(b // nb,),
        in_specs=[
            pl.BlockSpec((nb, hist, d), lambda i: (i, 0, 0)),
            pl.BlockSpec((rows, d), lambda i: (i, 0)),
            pl.BlockSpec((CONV_W, d), lambda i: (0, 0)),
            pl.BlockSpec((1, d), lambda i: (0, 0)),
            pl.BlockSpec((1, d), lambda i: (0, 0)),
            pl.BlockSpec((1, d), lambda i: (0, 0)),
        ],
        out_specs=(pl.BlockSpec((rows, d), lambda i: (i, 0)),
                   pl.BlockSpec((nb, hist, d), lambda i: (i, 0, 0))),
        scratch_shapes=[pltpu.VMEM((CONV_HIST + t_new + SUBLANES, d), F32), pltpu.VMEM((rows, d), F32)],
        compiler_params=_params("arbitrary"),
        name="conv_sample",
    )(state, glu, w_dw, b_dw, g_ln, b_ln)


def _mm_norm_res_kernel(*refs, has_bias):
    if has_bias:
        a_ref, w_ref, b_ref, g_ref, r_ref, o_ref = refs
    else:
        a_ref, w_ref, g_ref, r_ref, o_ref = refs
    m = jnp.dot(a_ref[...].astype(BF16), w_ref[...], preferred_element_type=F32)
    if has_bias:
        m = m + b_ref[...]
    o_ref[...] = r_ref[...] + _rms_scale(m) * g_ref[...]


def _mm_norm_res(a, w, bias, g, res, *, tm):
    m, k = a.shape
    d = w.shape[1]
    tm = _tile(m, tm)
    row = pl.BlockSpec((1, d), lambda i: (0, 0))
    has_bias = bias is not None
    in_specs = [pl.BlockSpec((tm, k), lambda i: (i, 0)), pl.BlockSpec((k, d), lambda i: (0, 0))]
    args = [a, w]
    if has_bias:
        in_specs.append(row)
        args.append(bias)
    in_specs += [row, pl.BlockSpec((tm, d), lambda i: (i, 0))]
    args += [g, res]
    return pl.pallas_call(
        functools.partial(_mm_norm_res_kernel, has_bias=has_bias),
        out_shape=jax.ShapeDtypeStruct((m, d), F32),
        grid=(m // tm,),
        in_specs=in_specs,
        out_specs=pl.BlockSpec((tm, d), lambda i: (i, 0)),
        compiler_params=_params("arbitrary"),
        name="mm_norm_res",
    )(*args)


def _ffn_kernel(h_ref, gpre_ref, wg_ref, wu_ref, wd_ref, gpost_ref, o_ref, z_ref, acc_ref):
    c = pl.program_id(1)

    @pl.when(c == 0)
    def _():
        z_ref[...] = (_rms_scale(h_ref[...]) * gpre_ref[...]).astype(BF16)
        acc_ref[...] = jnp.zeros(acc_ref.shape, F32)

    z = z_ref[...]
    gate = jnp.dot(z, wg_ref[...], preferred_element_type=F32)
    up = jnp.dot(z, wu_ref[...], preferred_element_type=F32)
    act = (gate * _sigmoid(gate) * up).astype(BF16)
    acc_ref[...] += jnp.dot(act, wd_ref[...], preferred_element_type=F32)

    @pl.when(c == pl.num_programs(1) - 1)
    def _():
        o_ref[...] = h_ref[...] + _rms_scale(acc_ref[...]) * gpost_ref[...]


def _ffn(h, g_pre, wg, wu, wd, g_post, *, layer, tm, tf):
    m, d = h.shape
    f = wg.shape[2]
    tm, tf = _tile(m, tm), _tile(f, tf)
    row = pl.BlockSpec((1, d), lambda i, c: (0, 0))
    return pl.pallas_call(
        _ffn_kernel,
        out_shape=jax.ShapeDtypeStruct((m, d), F32),
        grid=(m // tm, f // tf),
        in_specs=[
            pl.BlockSpec((tm, d), lambda i, c: (i, 0)),
            row,
            pl.BlockSpec((None, d, tf), lambda i, c: (layer, 0, c)),
            pl.BlockSpec((None, d, tf), lambda i, c: (layer, 0, c)),
            pl.BlockSpec((None, tf, d), lambda i, c: (layer, c, 0)),
            row,
        ],
        out_specs=pl.BlockSpec((tm, d), lambda i, c: (i, 0)),
        scratch_shapes=[pltpu.VMEM((tm, d), BF16), pltpu.VMEM((tm, d), F32)],
        compiler_params=_params("arbitrary", "arbitrary"),
        name="ffn",
    )(h, g_pre, wg, wu, wd, g_post)


def _ple_kernel(h_ref, p_ref, wgate_ref, wproj_ref, g_ref, o_ref):
    h = h_ref[...]
    gate = _sigmoid(jnp.dot(h.astype(BF16), wgate_ref[...], preferred_element_type=F32))
    e = jnp.dot(p_ref[...].astype(BF16), wproj_ref[...], preferred_element_type=F32)
    o_ref[...] = h + _rms_scale(gate * e) * g_ref[...]


def _ple(h, p, w_gate, w_proj, g, *, tm):
    m, d = h.shape
    pd = p.shape[1]
    tm = _tile(m, tm)
    return pl.pallas_call(
        _ple_kernel,
        out_shape=jax.ShapeDtypeStruct((m, d), F32),
        grid=(m // tm,),
        in_specs=[
            pl.BlockSpec((tm, d), lambda i: (i, 0)),
            pl.BlockSpec((tm, pd), lambda i: (i, 0)),
            pl.BlockSpec((d, d), lambda i: (0, 0)),
            pl.BlockSpec((pd, d), lambda i: (0, 0)),
            pl.BlockSpec((1, d), lambda i: (0, 0)),
        ],
        out_specs=pl.BlockSpec((tm, d), lambda i: (i, 0)),
        compiler_params=_params("arbitrary"),
        name="ple",
    )(h, p, w_gate, w_proj, g)


def _rope(y, cos, sin_signed):
    lane = lax.broadcasted_iota(jnp.int32, (y.shape[0], LANES), 1)
    low_half = (lane % HEAD_DIM) < ROPE_HALF
    out = []
    for lb in range(y.shape[1] // LANES):
        yb = y[:, lb * LANES:(lb + 1) * LANES]
        partner = jnp.where(low_half, pltpu.roll(yb, LANES - ROPE_HALF, 1), pltpu.roll(yb, ROPE_HALF, 1))
        out.append(yb * cos + partner * sin_signed)
    return out


def _qkv_kernel(h_ref, gu_ref, gs_ref, wq_ref, wk_ref, wv_ref, cos_ref, sin_ref,
                q_ref, k_ref, v_ref, u_ref, s_ref):
    @pl.when(pl.program_id(1) == 0)
    def _():
        r = _rms_scale(h_ref[...])
        u_ref[...] = (r * gu_ref[...]).astype(BF16)
        s_ref[...] = (r * gs_ref[...]).astype(BF16)

    cos, sin_signed = cos_ref[...], sin_ref[...]
    s = s_ref[...]
    q = _rope(jnp.dot(u_ref[...], wq_ref[...], preferred_element_type=F32), cos, sin_signed)
    k = _rope(jnp.dot(s, wk_ref[...], preferred_element_type=F32), cos, sin_signed)
    for lb in range(len(q)):
        lanes = slice(lb * LANES, (lb + 1) * LANES)
        q_ref[:, lanes] = (q[lb] * SCORE_SCALE).astype(q_ref.dtype)
        k_ref[:, lanes] = k[lb]
    v_ref[...] = jnp.dot(s, wv_ref[...], preferred_element_type=F32)


def _qkv(h, g_u, g_s, wq, wk, wv, cos, sin_signed, *, tm, tn, q_dtype):
    m, d = h.shape
    tm, tn = _tile(m, tm), _tile(d, tn)
    row = pl.BlockSpec((1, d), lambda i, j: (0, 0))
    wcol = pl.BlockSpec((d, tn), lambda i, j: (0, j))
    tab = pl.BlockSpec((tm, LANES), lambda i, j: (i, 0))
    out = pl.BlockSpec((tm, tn), lambda i, j: (i, j))
    return pl.pallas_call(
        _qkv_kernel,
        out_shape=(jax.ShapeDtypeStruct((m, d), q_dtype), jax.ShapeDtypeStruct((m, d), F32),
                   jax.ShapeDtypeStruct((m, d), F32)),
        grid=(m // tm, d // tn),
        in_specs=[pl.BlockSpec((tm, d), lambda i, j: (i, 0)), row, row, wcol, wcol, wcol, tab, tab],
        out_specs=(out, out, out),
        scratch_shapes=[pltpu.VMEM((tm, d), BF16), pltpu.VMEM((tm, d), BF16)],
        compiler_params=_params("arbitrary", "arbitrary"),
        name="qkv_rope",
    )(h, g_u, g_s, wq, wk, wv, cos, sin_signed)


def _rope_tables(pos):
    inv = 1.0 / (ROPE_THETA ** (jnp.arange(ROPE_HALF, dtype=F32) / ROPE_HALF))
    ang = pos.astype(F32)[:, None] * inv[None, :]
    cos, sin = jnp.cos(ang), jnp.sin(ang)
    reps = LANES // HEAD_DIM
    cos_l = jnp.tile(jnp.concatenate([cos, cos], axis=-1), (1, reps))
    sin_l = jnp.tile(jnp.concatenate([-sin, sin], axis=-1), (1, reps))
    return cos_l, sin_l


def _lambda(lam_ref, lam_init):
    lam = lam_ref[...]
    d1 = jnp.sum(lam[0:1] * lam[1:2], axis=-1, keepdims=True)
    d2 = jnp.sum(lam[2:3] * lam[3:4], axis=-1, keepdims=True)
    return jnp.exp(d1) - jnp.exp(d2) + lam_init


ROW_BLOCKS = 8


def _attn_prompt_kernel(q_ref, k_ref, v_ref, lam_ref, gsub_ref, o_ref, kb_ref, vb_ref, *, lam_init):
    tq = q_ref.shape[0]
    i = pl.program_id(2)
    nt = (((1,), (1,)), ((), ()))

    @pl.when(i == 0)
    def _():
        kb_ref[...] = k_ref[...].astype(BF16)
        vb_ref[:, 0:HEAD_W] = v_ref[...].astype(BF16)
        vb_ref[:, HEAD_W:] = jnp.ones((vb_ref.shape[0], HEAD_W), BF16)

    def attend(tile):
        past = tile * tq
        tr = tq // ROW_BLOCKS
        row = lax.broadcasted_iota(jnp.int32, (2 * tr, tr), 0) % tr
        col = lax.broadcasted_iota(jnp.int32, (2 * tr, tr), 1)
        lam = _lambda(lam_ref, lam_init)
        scores = []
        for rb in range(ROW_BLOCKS):
            before = past + rb * tr
            q = q_ref[rb * tr:(rb + 1) * tr, :]
            lane = lax.broadcasted_iota(jnp.int32, q.shape, 1)
            zero = jnp.zeros_like(q)
            q2 = jnp.concatenate([jnp.where(lane < HEAD_DIM, q, zero), jnp.where(lane >= HEAD_DIM, q, zero)],
                                 axis=0)
            s_diag = lax.dot_general(q2, kb_ref[before:before + tr, :], nt, preferred_element_type=F32)
            s_diag = jnp.where(col <= row, s_diag, -jnp.inf)
            s_past = (lax.dot_general(q2, kb_ref[0:before, :], nt, preferred_element_type=F32)
                      if before else None)
            scores.append((s_diag, s_past))
        for rb, (s_diag, s_past) in enumerate(scores):
            before = past + rb * tr
            m = jnp.max(s_diag, axis=-1, keepdims=True)
            if before:
                m = jnp.maximum(m, jnp.max(s_past, axis=-1, keepdims=True))
            acc = jnp.dot(jnp.exp2(s_diag - m).astype(BF16), vb_ref[before:before + tr, :],
                          preferred_element_type=F32)
            if before:
                acc = acc + jnp.dot(jnp.exp2(s_past - m).astype(BF16), vb_ref[0:before, :],
                                    preferred_element_type=F32)
            out = acc[:, 0:HEAD_W] / acc[:, HEAD_W:]
            o = out[0:tr] - lam * out[tr:]
            o_ref[rb * tr:(rb + 1) * tr, :] = (
                _rms_scale(o) * gsub_ref[...] * (1.0 - lam_init)).astype(o_ref.dtype)

    for tile in range(kb_ref.shape[0] // tq):
        pl.when(i == tile)(functools.partial(attend, tile))


def _attn_prompt(q, k, v, lam_vecs, g_sub, *, batch, seq, tq, lam_init):
    m, d = q.shape
    heads = d // HEAD_W
    tq = _tile(seq, tq)
    nq = seq // tq
    return pl.pallas_call(
        functools.partial(_attn_prompt_kernel, lam_init=lam_init),
        out_shape=jax.ShapeDtypeStruct((m, d), BF16),
        grid=(batch, heads, nq),
        in_specs=[
            pl.BlockSpec((tq, HEAD_W), lambda b, h, i: (b * nq + i, h)),
            pl.BlockSpec((seq, HEAD_W), lambda b, h, i: (b, h)),
            pl.BlockSpec((seq, HEAD_W), lambda b, h, i: (b, h)),
            pl.BlockSpec(lam_vecs.shape, lambda b, h, i: (0, 0)),
            pl.BlockSpec((1, HEAD_W), lambda b, h, i: (0, 0)),
        ],
        out_specs=pl.BlockSpec((tq, HEAD_W), lambda b, h, i: (b * nq + i, h)),
        scratch_shapes=[pltpu.VMEM((seq, HEAD_W), BF16), pltpu.VMEM((seq, 2 * HEAD_W), BF16)],
        compiler_params=_params("arbitrary", "arbitrary", "arbitrary"),
        name="attn_prompt",
    )(q, k, v, lam_vecs, g_sub)


PAGES_PER_STEP = 4
PAGE_SLOTS = 3


def _attn_sample_kernel(pt_ref, q_ref, kn_ref, vn_ref, ck_hbm, cv_hbm, lam_ref, gsub_ref, o_ref,
                        kbuf_ref, vbuf_ref, sem_ref, qrows_ref, bias_ref, m_ref, l_ref, acc_ref,
                        *, lam_init):
    t_new, heads = q_ref.shape[1], q_ref.shape[2]
    rows = 2 * t_new * heads
    n_group, page = kbuf_ref.shape[1], kbuf_ref.shape[2]
    step = pl.program_id(1)
    nt = (((1,), (1,)), ((), ()))

    n_steps = pl.num_programs(0) * pl.num_programs(1)
    cur = pl.program_id(0) * pl.num_programs(1) + step

    def page_copies(group):
        slot = group % PAGE_SLOTS
        copies = []
        for g in range(n_group):
            pg = pt_ref[group * n_group + g]
            copies.append(pltpu.make_async_copy(ck_hbm.at[pg], kbuf_ref.at[slot, g], sem_ref.at[slot, 0, g]))
            copies.append(pltpu.make_async_copy(cv_hbm.at[pg], vbuf_ref.at[slot, g], sem_ref.at[slot, 1, g]))
        return copies

    def start(group):
        for c in page_copies(group):
            c.start()

    @pl.when(cur == 0)
    def _():
        for group in range(PAGE_SLOTS - 1):
            start(group)

    @pl.when(cur + (PAGE_SLOTS - 1) < n_steps)
    def _():
        start(cur + (PAGE_SLOTS - 1))

    for c in page_copies(cur):
        c.wait()
    slot = cur % PAGE_SLOTS

    @pl.when(step == 0)
    def _():
        lane = lax.broadcasted_iota(jnp.int32, (heads, HEAD_W), 1)
        for c in range(2):
            own = (lane // HEAD_DIM) == c
            for t in range(t_new):
                r0 = (c * t_new + t) * heads
                qrows_ref[r0:r0 + heads, :] = jnp.where(own, q_ref[0, t], 0.0).astype(BF16)
        row_head = lax.broadcasted_iota(jnp.int32, bias_ref.shape, 0) % heads
        key_head = lax.broadcasted_iota(jnp.int32, bias_ref.shape, 1) % heads
        bias_ref[...] = jnp.where(row_head == key_head, 0.0, -jnp.inf)
        m_ref[...] = jnp.full(m_ref.shape, -jnp.inf, F32)
        l_ref[...] = jnp.zeros(l_ref.shape, F32)
        acc_ref[...] = jnp.zeros(acc_ref.shape, F32)

    def partial_softmax(s, v):
        m = jnp.max(s, axis=-1, keepdims=True)
        p = jnp.exp2(s - m)
        return m, jnp.sum(p, axis=-1, keepdims=True), jnp.dot(p.astype(BF16), v, preferred_element_type=F32)

    def merge(parts):
        m_prev = m_ref[...]
        m_new = m_prev
        for m, _, _ in parts:
            m_new = jnp.maximum(m_new, m)
        alpha = jnp.exp2(m_prev - m_new)
        l_new = alpha * l_ref[...]
        acc = alpha * acc_ref[...]
        for m, l, a in parts:
            w = jnp.exp2(m - m_new)
            l_new = l_new + w * l
            acc = acc + w * a
        m_ref[...], l_ref[...], acc_ref[...] = m_new, l_new, acc

    q_rows = qrows_ref[...]
    bias = bias_ref[...]
    scores = []
    for g in range(n_group):
        keys = kbuf_ref[slot, g].reshape(page * heads, HEAD_W).astype(BF16)
        scores.append(lax.dot_general(q_rows, keys, nt, preferred_element_type=F32) + bias)
    maxes = [jnp.max(s, axis=-1, keepdims=True) for s in scores]
    probs = [jnp.exp2(s - m) for s, m in zip(scores, maxes)]
    sums = [jnp.sum(p, axis=-1, keepdims=True) for p in probs]
    outs = [jnp.dot(p.astype(BF16), vbuf_ref[slot, g].reshape(page * heads, HEAD_W).astype(BF16),
                    preferred_element_type=F32) for g, p in enumerate(probs)]
    merge(list(zip(maxes, sums, outs)))

    @pl.when(step == pl.num_programs(1) - 1)
    def _():
        keys = kn_ref[0].reshape(t_new * heads, HEAD_W).astype(BF16)
        s = lax.dot_general(q_rows, keys, nt, preferred_element_type=F32)
        row = lax.broadcasted_iota(jnp.int32, s.shape, 0)
        col = lax.broadcasted_iota(jnp.int32, s.shape, 1)
        ok = (row % heads == col % heads) & (col // heads <= (row // heads) % t_new)
        merge([partial_softmax(jnp.where(ok, s, -jnp.inf),
                               vn_ref[0].reshape(t_new * heads, HEAD_W).astype(BF16))])

        half = rows // 2
        lam = _lambda(lam_ref, lam_init)
        o = acc_ref[...] / l_ref[...]
        w = o[0:half] - lam * o[half:rows]
        w = _rms_scale(w) * gsub_ref[...] * (1.0 - lam_init)
        o_ref[0] = w.reshape(t_new, heads, HEAD_W)


def _attn_sample(q, k_new, v_new, cache_k, cache_v, page_table, lam_vecs, g_sub, *, lam_init):
    b, t_new, heads, _ = q.shape
    n_pages = page_table.shape[1]
    page = cache_k.shape[1]
    rows = 2 * t_new * heads
    n_group = math.gcd(n_pages, PAGES_PER_STEP)
    steps = n_pages // n_group
    assert b * steps >= PAGE_SLOTS
    tok = pl.BlockSpec((1, t_new, heads, HEAD_W), lambda bi, s, pt: (bi, 0, 0, 0))
    hbm = pl.BlockSpec(memory_space=pl.ANY)
    grid_spec = pltpu.PrefetchScalarGridSpec(
        num_scalar_prefetch=1,
        grid=(b, steps),
        in_specs=[tok, tok, tok, hbm, hbm,
                  pl.BlockSpec(lam_vecs.shape, lambda bi, s, pt: (0, 0)),
                  pl.BlockSpec((1, HEAD_W), lambda bi, s, pt: (0, 0))],
        out_specs=tok,
        scratch_shapes=[pltpu.VMEM((PAGE_SLOTS, n_group, page, heads, HEAD_W), cache_k.dtype),
                        pltpu.VMEM((PAGE_SLOTS, n_group, page, heads, HEAD_W), cache_v.dtype),
                        pltpu.SemaphoreType.DMA((PAGE_SLOTS, 2, n_group)),
                        pltpu.VMEM((rows, HEAD_W), BF16), pltpu.VMEM((rows, page * heads), F32),
                        pltpu.VMEM((rows, 1), F32), pltpu.VMEM((rows, 1), F32),
                        pltpu.VMEM((rows, HEAD_W), F32)],
    )
    return pl.pallas_call(
        functools.partial(_attn_sample_kernel, lam_init=lam_init),
        out_shape=jax.ShapeDtypeStruct((b, t_new, heads, HEAD_W), F32),
        grid_spec=grid_spec,
        compiler_params=_params("arbitrary", "arbitrary"),
        name="attn_sample",
    )(page_table.reshape(-1), q, k_new, v_new, cache_k, cache_v, lam_vecs, g_sub)


def _row(v):
    return v.reshape(1, -1).astype(F32)


def _post_mixer(h, p, i, W, tm):
    h = _ffn(h, _row(W['g_ffn_pre'][i]), W['w_ffn_gate'], W['w_ffn_up'], W['w_ffn_down'],
             _row(W['g_ffn_post'][i]), layer=i, tm=tm, tf=512)
    return _ple(h, p[i], W['w_ple_gate'][i], W['w_ple_proj'][i], _row(W['g_ple_post'][i]), tm=tm)


def _conv_layer(x, conv_fn, W, tm):
    glu = _pw1_glu(x, _row(W['g_mix_pre'][0]), W['w_pw1'][0], _row(W['b_pw1'][0]),
                   tm=ROW_TILE_STREAMED, tn=512)
    act, extra = conv_fn(glu)
    h = _mm_norm_res(act, W['w_pw2'][0], _row(W['b_pw2'][0]), _row(W['g_mix_post'][0]), x, tm=tm)
    return h, extra


def kernel(x_prompt, x_sample, p_prompt, p_sample, state_conv, cache_k, cache_v, page_table, g_mix_pre, g_mix_post, g_ffn_pre, g_ffn_post, g_ple_post, w_pw1, b_pw1, w_dw, b_dw, g_conv_ln, b_conv_ln, w_pw2, b_pw2, g_kv_norm, w_k, w_v, w_q, lam_q1, lam_k1, lam_q2, lam_k2, g_subln, w_o, w_ffn_gate, w_ffn_up, w_ffn_down, w_ple_proj, w_ple_gate):
    B, S, D = x_prompt.shape
    DB, T, _ = x_sample.shape
    heads = D // HEAD_W
    page = cache_k.shape[1]
    assert state_conv.shape[0] == 1 and w_q.shape[0] == 1, "one conv layer then one attention layer"

    W = {
        'g_mix_pre': g_mix_pre, 'g_mix_post': g_mix_post, 'g_ffn_pre': g_ffn_pre,
        'g_ffn_post': g_ffn_post, 'g_ple_post': g_ple_post, 'b_pw1': b_pw1, 'b_pw2': b_pw2,
    }
    for name, w in (('w_pw1', w_pw1), ('w_pw2', w_pw2), ('w_ple_proj', w_ple_proj), ('w_ple_gate', w_ple_gate)):
        W[name] = [w[i].astype(BF16) for i in range(w.shape[0])]
    for name, w in (('w_ffn_gate', w_ffn_gate), ('w_ffn_up', w_ffn_up), ('w_ffn_down', w_ffn_down)):
        W[name] = w.astype(BF16)
    wq, wk, wv, wo = w_q[0].astype(BF16), w_k.astype(BF16), w_v.astype(BF16), w_o[0].astype(BF16)
    w_dw0, b_dw0 = w_dw[0], _row(b_dw[0])
    g_ln0, b_ln0 = _row(g_conv_ln[0]), _row(b_conv_ln[0])
    lam_vecs = jnp.concatenate([lam_q1, lam_k1, lam_q2, lam_k2], axis=0).astype(F32)
    lam_init = 0.8 - 0.6 * math.exp(-0.3 * 1)
    g_sub = _row(g_subln[0])

    def attention_layer(h, pos, attn_fn, tm, q_dtype):
        cos, sin_signed = _rope_tables(pos)
        q, k, v = _qkv(h, _row(g_mix_pre[1]), _row(g_kv_norm), wq, wk, wv, cos, sin_signed,
                       tm=tm, tn=512, q_dtype=q_dtype)
        o = attn_fn(q, k, v)
        h = _mm_norm_res(o, wo, None, _row(g_mix_post[1]), h, tm=tm)
        return h, k, v

    tm_p = 512
    xp = x_prompt.reshape(B * S, D)
    pp = p_prompt.reshape(p_prompt.shape[0], B * S, -1)

    def conv_p(glu):
        act = _conv_prompt(glu, w_dw0, b_dw0, g_ln0, b_ln0, seq=S, tm=tm_p)
        return act, glu.reshape(B, S, D)[:, S - CONV_HIST:, :][None]

    h, conv_state_p = _conv_layer(xp, conv_p, W, tm_p)
    h = _post_mixer(h, pp, 0, W, tm_p)
    h, k_p, v_p = attention_layer(
        h, jnp.tile(jnp.arange(S), B),
        lambda q, k, v: _attn_prompt(q, k, v, lam_vecs, g_sub, batch=B, seq=S, tq=2048, lam_init=lam_init),
        tm_p, BF16)
    y_p = _post_mixer(h, pp, 1, W, tm_p).reshape(B, S, D)

    tm_s = 512
    xs = x_sample.reshape(DB * T, D)
    ps = p_sample.reshape(p_sample.shape[0], DB * T, -1)
    past_len = page_table.shape[1] * page

    def conv_s(glu):
        return _conv_sample(state_conv[0], glu, w_dw0, b_dw0, g_ln0, b_ln0, nb=8)

    def attn_s(q, k, v):
        per_head = (DB, T, heads, HEAD_W)
        o = _attn_sample(q.reshape(per_head), k.reshape(per_head), v.reshape(per_head),
                         cache_k, cache_v, page_table, lam_vecs, g_sub, lam_init=lam_init)
        return o.reshape(DB * T, D)

    h, conv_state_s = _conv_layer(xs, conv_s, W, tm_s)
    h = _post_mixer(h, ps, 0, W, tm_s)
    h, k_s, v_s = attention_layer(h, jnp.tile(past_len + jnp.arange(T), DB), attn_s, tm_s, F32)
    y_s = _post_mixer(h, ps, 1, W, tm_s).reshape(DB, T, D)

    return (y_p, y_s, conv_state_p,
            k_p.reshape(B, S, heads, HEAD_W), v_p.reshape(B, S, heads, HEAD_W),
            conv_state_s[None],
            k_s.reshape(DB, T, heads, HEAD_W), v_s.reshape(DB, T, heads, HEAD_W))
```
